```python
import math
import jax, jax.numpy as jnp
from jax import lax
import numpy as np


D_MODEL = 2048
BATCH = 16
SEQ = 2048
DEPTH = 1
DEC_BATCH = 4
DEC_SEQ = 8192
PAST_LEN = 128

N_MEM = 256
A_WIDTH = D_MODEL // 2
A_HALF_DIM = 64
A_HEADS = A_WIDTH // (2 * A_HALF_DIM)
A_VDIM = 2 * A_HALF_DIM
B_WIDTH = D_MODEL // 2
B_DIM = 128
B_HEADS = B_WIDTH // B_DIM
CHUNK = 64
REL_BUCKETS = 32
REL_MAX_DIST = 128
Q_BLOCK = 128
X_HEADS = 4
X_DIM = 128
X_WIDTH = X_HEADS * X_DIM
D_FF = -(-8 * D_MODEL // (3 * 256)) * 256
RMS_EPS = 1e-6
IN_SIZES = (A_WIDTH, A_WIDTH, A_WIDTH, B_WIDTH, B_WIDTH, B_WIDTH, B_WIDTH, B_WIDTH, 2 * D_MODEL)
IN_SPLITS = tuple(int(c) for c in np.cumsum(IN_SIZES)[:-1])
N_IN = int(sum(IN_SIZES))

kernel_name = 'hybrid_diffattn_hgrn2_encoder'


def rmsnorm(x, g, eps=RMS_EPS):
    xf = x.astype(jnp.float32)
    y = xf * lax.rsqrt(jnp.mean(xf * xf, axis=-1, keepdims=True) + eps)
    return (y * g.astype(jnp.float32)).astype(x.dtype)


def rel_bucket(rel):
    nb = REL_BUCKETS // 2
    max_exact = nb // 2
    n = jnp.abs(rel)
    nf = jnp.maximum(n, 1).astype(jnp.float32)
    large = max_exact + (jnp.log(nf / max_exact) / math.log(REL_MAX_DIST / max_exact) * (nb - max_exact)).astype(jnp.int32)
    large = jnp.minimum(large, nb - 1)
    return jnp.where(rel > 0, nb, 0) + jnp.where(n < max_exact, n, large)


def diff_attention(q, k, v, rel_bias, lam, g_subln, lam_init):
    bsz, s_len = q.shape[0], q.shape[1]
    nblk = s_len // Q_BLOCK
    q_blocks = jnp.swapaxes(q.reshape(bsz, nblk, Q_BLOCK, A_HEADS, 2, A_HALF_DIM), 0, 1)
    k_pos = jnp.arange(s_len, dtype=jnp.int32)
    scale = A_HALF_DIM ** -0.5

    def block(args):
        q_blk, start = args
        q_pos = start + jnp.arange(Q_BLOCK, dtype=jnp.int32)
        bias = jnp.transpose(rel_bias[rel_bucket(k_pos[None, :] - q_pos[:, None])], (2, 0, 1)).astype(jnp.float32)
        logits = jnp.einsum('bqhcd,bkhcd->bhcqk', q_blk, k).astype(jnp.float32) * scale + bias[None, :, None]
        p = jax.nn.softmax(logits, axis=-1)
        w = p[:, :, 0] - lam * p[:, :, 1]
        return jnp.einsum('bhqk,bkhe->bqhe', w.astype(v.dtype), v)

    starts = jnp.arange(nblk, dtype=jnp.int32) * Q_BLOCK
    o = lax.map(block, (q_blocks, starts))
    o = jnp.swapaxes(o, 0, 1).reshape(bsz, s_len, A_HEADS, A_VDIM)
    o = rmsnorm(o, g_subln) * (1.0 - lam_init)
    return o.reshape(bsz, s_len, A_WIDTH)


def hgrn_chunk_scan(q, k, v, logf):
    bsz, s_len = q.shape[0], q.shape[1]
    n = s_len // CHUNK
    rs = lambda t: t.reshape(bsz, n, CHUNK, B_HEADS, t.shape[-1])
    q, k, v, logf = rs(q), rs(k), rs(v), rs(logf)
    b = jnp.cumsum(logf, axis=2)
    b_ref = b[:, :, CHUNK // 2 - 1:CHUNK // 2]
    b_last = b[:, :, -1:]
    a = jnp.einsum('bnthd,bnshd->bnhts', q * jnp.exp(b - b_ref), k * jnp.exp(b_ref - b))
    tri = jnp.tril(jnp.ones((CHUNK, CHUNK), dtype=bool))
    a = jnp.where(tri, a, 0.0)
    intra = jnp.einsum('bnhts,bnshv->bnthv', a, v)
    kv = jnp.einsum('bnshd,bnshv->bnhdv', k * jnp.exp(b_last - b), v)
    decay = jnp.exp(b_last[:, :, 0])

    def step(state, xs):
        kv_n, dec_n = xs
        return state * dec_n[..., None] + kv_n, state

    s0 = jnp.zeros((bsz, B_HEADS, B_DIM, v.shape[-1]), jnp.float32)
    _, s_prev = lax.scan(step, s0, (jnp.moveaxis(kv, 1, 0), jnp.moveaxis(decay, 1, 0)))
    s_prev = jnp.moveaxis(s_prev, 0, 1)
    inter = jnp.einsum('bnthd,bnhdv->bnthv', q * jnp.exp(b), s_prev)
    return (intra + inter).reshape(bsz, s_len, B_HEADS, v.shape[-1])


def hgrn2_branch(q_in, z_fwd, z_bwd, v_in, g_in, lb_fwd, lb_bwd, g_norm):
    bsz, s_len = q_in.shape[0], q_in.shape[1]
    heads = lambda t: t.reshape(bsz, s_len, B_HEADS, B_DIM)
    q = heads(q_in.astype(jnp.float32)) * B_DIM ** -0.5
    v = heads(v_in.astype(jnp.float32))

    def gates(z, lb):
        z = z.astype(jnp.float32)
        logf = jnp.log(lb + (1.0 - lb) * jax.nn.sigmoid(z))
        k = (1.0 - lb) * jax.nn.sigmoid(-z)
        return heads(k), heads(logf)

    k_f, logf_f = gates(z_fwd, lb_fwd)
    k_b, logf_b = gates(z_bwd, lb_bwd)
    flip = lambda t: jnp.flip(t, axis=1)
    o = hgrn_chunk_scan(q, k_f, v, logf_f) + flip(hgrn_chunk_scan(flip(q), flip(k_b), flip(v), flip(logf_b)))
    o = rmsnorm(o, g_norm) * jax.nn.silu(heads(g_in.astype(jnp.float32)))
    return o.reshape(bsz, s_len, B_WIDTH).astype(q_in.dtype)


def encoder_layer(x, mem, l, p, lb_all):
    dt = x.dtype
    bsz, s_len = x.shape[0], x.shape[1]
    h = rmsnorm(x, p['g_pre_mix'][l])
    proj = h @ p['w_in'][l]
    qa, ka, va, qb, zf, zb, vb, gb, gate_logits = jnp.split(proj, IN_SPLITS, axis=-1)
    lam_init = 0.8 - 0.6 * math.exp(-0.3 * l)
    lam = (jnp.exp(jnp.sum(p['lam_q1'][l].astype(jnp.float32) * p['lam_k1'][l].astype(jnp.float32)))
           - jnp.exp(jnp.sum(p['lam_q2'][l].astype(jnp.float32) * p['lam_k2'][l].astype(jnp.float32))) + lam_init)
    shp = (bsz, s_len, A_HEADS, 2, A_HALF_DIM)
    ya = diff_attention(qa.reshape(shp), ka.reshape(shp), va.reshape(bsz, s_len, A_HEADS, A_VDIM),
                        p['rel_bias'], lam, p['g_subln'][l], lam_init)
    yb = hgrn2_branch(qb, zf, zb, vb, gb, lb_all[0, l], lb_all[1, l], p['g_hgrn_norm'][l])
    g_a, g_b = jnp.split(jax.nn.sigmoid(gate_logits + p['b_merge'][l]), 2, axis=-1)
    mix = g_a * (ya @ p['w_branch_a'][l]) + g_b * (yb @ p['w_branch_b'][l])
    x = x + rmsnorm(mix @ p['w_out'][l], p['g_post_mix'][l])
    h = rmsnorm(x, p['g_pre_x'][l])
    m = rmsnorm(mem, p['g_mem'][l])
    qx = (h @ p['w_q_x'][l]).reshape(bsz, s_len, X_HEADS, X_DIM)
    kv = (m @ p['w_kv_x'][l]).reshape(bsz, mem.shape[1], 2, X_HEADS, X_DIM)
    logits = jnp.einsum('bqhd,bkhd->bhqk', qx, kv[:, :, 0]).astype(jnp.float32) * X_DIM ** -0.5
    att = jax.nn.softmax(logits, axis=-1).astype(dt)
    ox = jnp.einsum('bhqk,bkhd->bqhd', att, kv[:, :, 1]).reshape(bsz, s_len, X_WIDTH)
    x = x + rmsnorm(ox @ p['w_o_x'][l], p['g_post_x'][l])
    h = rmsnorm(x, p['g_pre_ffn'][l])
    gt, up = jnp.split(h @ p['w_gate_up'][l], 2, axis=-1)
    x = x + rmsnorm((jax.nn.silu(gt) * up) @ p['w_down'][l], p['g_post_ffn'][l])
    return x


def trunk(x, mem, p):
    lb_all = jnp.cumsum(jax.nn.softmax(p['hgrn_lb_logits'].astype(jnp.float32), axis=1), axis=1)
    for l in range(DEPTH):
        x = encoder_layer(x, mem, l, p, lb_all)
    return x


def setup_inputs(seed: int = 0) -> dict:
    key = jax.random.key(seed)
    ks = iter(jax.random.split(key, 40))
    nrm = lambda shape, scale: jax.random.normal(next(ks), shape, jnp.float32) * scale
    gain = lambda shape: 1.0 + nrm(shape, 0.05)
    return {
        'x_prompt': nrm((BATCH, SEQ, D_MODEL), 1.0),
        'x_sample': nrm((DEC_BATCH, DEC_SEQ, D_MODEL), 1.0),
        'mem_prompt': nrm((BATCH, N_MEM, D_MODEL), 1.0),
        'mem_sample': nrm((DEC_BATCH, N_MEM, D_MODEL), 1.0),
        'rel_bias': nrm((REL_BUCKETS, A_HEADS), 0.5),
        'hgrn_lb_logits': nrm((2, DEPTH + 1, B_WIDTH), 0.1),
        'g_pre_mix': gain((DEPTH, D_MODEL)),
        'w_in': nrm((DEPTH, D_MODEL, N_IN), D_MODEL ** -0.5),
        'b_merge': nrm((DEPTH, 2 * D_MODEL), 0.1),
        'lam_q1': nrm((DEPTH, A_HALF_DIM), 0.1),
        'lam_k1': nrm((DEPTH, A_HALF_DIM), 0.1),
        'lam_q2': nrm((DEPTH, A_HALF_DIM), 0.1),
        'lam_k2': nrm((DEPTH, A_HALF_DIM), 0.1),
        'g_subln': gain((DEPTH, A_VDIM)),
        'g_hgrn_norm': gain((DEPTH, B_DIM)),
        'w_branch_a': nrm((DEPTH, A_WIDTH, D_MODEL), A_WIDTH ** -0.5),
        'w_branch_b': nrm((DEPTH, B_WIDTH, D_MODEL), B_WIDTH ** -0.5),
        'w_out': nrm((DEPTH, D_MODEL, D_MODEL), D_MODEL ** -0.5),
        'g_post_mix': gain((DEPTH, D_MODEL)),
        'g_pre_x': gain((DEPTH, D_MODEL)),
        'g_mem': gain((DEPTH, D_MODEL)),
        'w_q_x': nrm((DEPTH, D_MODEL, X_WIDTH), D_MODEL ** -0.5),
        'w_kv_x': nrm((DEPTH, D_MODEL, 2 * X_WIDTH), D_MODEL ** -0.5),
        'w_o_x': nrm((DEPTH, X_WIDTH, D_MODEL), X_WIDTH ** -0.5),
        'g_post_x': gain((DEPTH, D_MODEL)),
        'g_pre_ffn': gain((DEPTH, D_MODEL)),
        'w_gate_up': nrm((DEPTH, D_MODEL, 2 * D_FF), D_MODEL ** -0.5),
        'w_down': nrm((DEPTH, D_FF, D_MODEL), D_FF ** -0.5),
        'g_post_ffn': gain((DEPTH, D_MODEL)),
    }


def reference(x_prompt, x_sample, mem_prompt, mem_sample, rel_bias, hgrn_lb_logits, g_pre_mix, w_in, b_merge,
              lam_q1, lam_k1, lam_q2, lam_k2, g_subln, g_hgrn_norm, w_branch_a, w_branch_b, w_out, g_post_mix,
              g_pre_x, g_mem, w_q_x, w_kv_x, w_o_x, g_post_x, g_pre_ffn, w_gate_up, w_down, g_post_ffn):
    p = dict(rel_bias=rel_bias, hgrn_lb_logits=hgrn_lb_logits, g_pre_mix=g_pre_mix, w_in=w_in, b_merge=b_merge,
             lam_q1=lam_q1, lam_k1=lam_k1, lam_q2=lam_q2, lam_k2=lam_k2, g_subln=g_subln,
             g_hgrn_norm=g_hgrn_norm, w_branch_a=w_branch_a, w_branch_b=w_branch_b, w_out=w_out,
             g_post_mix=g_post_mix, g_pre_x=g_pre_x, g_mem=g_mem, w_q_x=w_q_x, w_kv_x=w_kv_x, w_o_x=w_o_x,
             g_post_x=g_post_x, g_pre_ffn=g_pre_ffn, w_gate_up=w_gate_up, w_down=w_down, g_post_ffn=g_post_ffn)
    y_prompt = trunk(x_prompt, mem_prompt, p)
    y_sample = trunk(x_sample, mem_sample, p)
    return (y_prompt, y_sample)
```

```python
import functools
import math

import jax
import jax.numpy as jnp
import numpy as np
from jax import lax
from jax.experimental import pallas as pl
from jax.experimental.pallas import tpu as pltpu

F32 = jnp.float32
BF16 = jnp.bfloat16

D_MODEL = 2048
A_WIDTH = 1024
A_HALF = 64
A_HEADS = 8
A_VDIM = 128
B_WIDTH = 1024
B_DIM = 128
B_HEADS = 8
CHUNK = 64
REL_BUCKETS = 32
REL_MAX_DIST = 128
X_HEADS = 4
X_DIM = 128
X_WIDTH = 512
D_FF = 5632
N_IN = 12288
EPS = 1e-6
LOG2E = 1.4426950408889634
LAM_INIT = 0.8 - 0.6 * math.exp(-0.3 * 0)

LANES = 128
VMEM_LIMIT = 56 * 1024 * 1024

COL_QA, COL_KA, COL_VA, COL_QB, COL_ZF, COL_ZB, COL_VB, COL_GB, COL_GATE = range(9)

NT = (((1,), (1,)), ((), ()))
TN = (((0,), (0,)), ((), ()))


def _rms(x, g):
    ms = jnp.mean(x * x, axis=-1, keepdims=True)
    return x * lax.rsqrt(ms + EPS) * g


def _sigmoid(x):
    return 1.0 / (1.0 + jnp.exp(-x))


def _inproj_kernel(x_ref, g_ref, w_ref, cb_ref, o_ref, h_ref):
    j = pl.program_id(1)

    @pl.when(j == 0)
    def _():
        h_ref[...] = _rms(x_ref[...], g_ref[...]).astype(BF16)

    acc = jnp.dot(h_ref[...], w_ref[...], preferred_element_type=F32)

    @pl.when(j == COL_QA)
    def _():
        o_ref[...] = (acc * (A_HALF ** -0.5 * LOG2E)).astype(BF16)

    @pl.when(jnp.logical_and(j > COL_QA, j < COL_GATE))
    def _():
        o_ref[...] = acc.astype(BF16)

    @pl.when(j >= COL_GATE)
    def _():
        o_ref[...] = _sigmoid(acc + cb_ref[...]).astype(BF16)


def _inproj(x2d, g, w, colbias, tm):
    t = x2d.shape[0]
    tn = 1024
    return pl.pallas_call(
        _inproj_kernel,
        grid=(t // tm, N_IN // tn),
        in_specs=[
            pl.BlockSpec((tm, D_MODEL), lambda i, j: (i, 0)),
            pl.BlockSpec((1, D_MODEL), lambda i, j: (0, 0)),
            pl.BlockSpec((D_MODEL, tn), lambda i, j: (0, j)),
            pl.BlockSpec((1, tn), lambda i, j: (0, j)),
        ],
        out_specs=pl.BlockSpec((tm, tn), lambda i, j: (i, j)),
        out_shape=jax.ShapeDtypeStruct((t, N_IN), BF16),
        scratch_shapes=[pltpu.VMEM((tm, D_MODEL), BF16)],
        compiler_params=pltpu.CompilerParams(
            dimension_semantics=("parallel", "arbitrary"), vmem_limit_bytes=VMEM_LIMIT),
        name="inproj",
    )(x2d, g, w, colbias)


def _memkv_kernel(m_ref, g_ref, w_ref, o_ref):
    h = _rms(m_ref[0], g_ref[...]).astype(BF16)
    o_ref[0] = jnp.dot(h, w_ref[...], preferred_element_type=F32).astype(BF16)


def _memkv(mem, g, w):
    b, n, _ = mem.shape
    return pl.pallas_call(
        _memkv_kernel,
        grid=(b,),
        in_specs=[
            pl.BlockSpec((1, n, D_MODEL), lambda i: (i, 0, 0)),
            pl.BlockSpec((1, D_MODEL), lambda i: (0, 0)),
            pl.BlockSpec((D_MODEL, 2 * X_WIDTH), lambda i: (0, 0)),
        ],
        out_specs=pl.BlockSpec((1, n, 2 * X_WIDTH), lambda i: (i, 0, 0)),
        out_shape=jax.ShapeDtypeStruct((b, n, 2 * X_WIDTH), BF16),
        compiler_params=pltpu.CompilerParams(
            dimension_semantics=("parallel",), vmem_limit_bytes=VMEM_LIMIT),
        name="memkv",
    )(mem, g, w)


def _attn_kernel(sc_ref, q_ref, k_ref, v_ref, strip_ref, g_ref, o_ref, vt_ref, *, seq, tk):
    h = pl.program_id(1)
    nk = seq // tk
    nq = seq // LANES
    r = tk // LANES
    c_left = sc_ref[h, 0]
    c_right = sc_ref[h, 1]
    lam = sc_ref[h, 2]
    post = sc_ref[h, 3]

    row = lax.broadcasted_iota(jnp.int32, (LANES, LANES), 0)
    lane = lax.broadcasted_iota(jnp.int32, (LANES, LANES), 1)
    eye = (row == lane).astype(BF16)

    def vt_body(j, carry):
        vc = v_ref[0, pl.ds(pl.multiple_of(j * tk, tk), tk), :]
        vt_ref[j] = lax.dot_general(eye, vc, NT, preferred_element_type=F32).astype(BF16)
        return carry

    lax.fori_loop(0, nk, vt_body, 0)

    def q_body(qi, carry0):
        q0 = pl.multiple_of(qi * LANES, LANES)
        qs = q_ref[0, pl.ds(q0, LANES), :]
        zero = jnp.zeros_like(qs)
        qst = jnp.concatenate([jnp.where(lane < A_HALF, qs, zero),
                               jnp.where(lane >= A_HALF, qs, zero)], axis=0)

        def chunk(j, carry, kind):
            m, l, acc = carry
            kc = k_ref[0, pl.ds(pl.multiple_of(j * tk, tk), tk), :]
            st = lax.dot_general(kc, qst, NT, preferred_element_type=F32)
            if kind == "near":
                start = pl.multiple_of((j * r - qi) * LANES + tk, LANES)
                b = strip_ref[0, pl.ds(start, tk), :]
                st = st + jnp.concatenate([b, b], axis=1)
                c = 0.0
            else:
                c = c_left if kind == "left" else c_right
            m_new = jnp.maximum(m, jnp.max(st, axis=0, keepdims=True) + c)
            alpha = jnp.exp2(m - m_new)
            p = jnp.exp2(st - (m_new - c))
            l = alpha * l + jnp.sum(p, axis=0, keepdims=True)
            pv = jnp.dot(vt_ref[j], p.astype(BF16), preferred_element_type=F32)
            return m_new, l, alpha * acc + pv

        jn0 = jnp.maximum(qi - 1, 0) // r
        jn1 = jnp.minimum((qi + r + 1) // r, nk)
        carry = (jnp.full((1, 2 * LANES), -1e30, F32), jnp.zeros((1, 2 * LANES), F32),
                 jnp.zeros((A_VDIM, 2 * LANES), F32))
        carry = lax.fori_loop(0, jn0, functools.partial(chunk, kind="left"), carry)
        carry = lax.fori_loop(jn0, jn1, functools.partial(chunk, kind="near"), carry)
        carry = lax.fori_loop(jn1, nk, functools.partial(chunk, kind="right"), carry)
        _, l, acc = carry
        o = acc * (1.0 / l)
        ot = o[:, :LANES] - lam * o[:, LANES:]
        ms = jnp.mean(ot * ot, axis=0, keepdims=True)
        y = ot * lax.rsqrt(ms + EPS) * g_ref[...] * post
        o_ref[0, pl.ds(q0, LANES), :] = y.T.astype(BF16)
        return carry0

    lax.fori_loop(0, nq, q_body, 0)


def _attention(p3, scal, strip, gcol, tk):
    b, s, _ = p3.shape
    kern = functools.partial(_attn_kernel, seq=s, tk=tk)
    return pl.pallas_call(
        kern,
        grid=(b, A_HEADS),
        in_specs=[
            pl.BlockSpec(memory_space=pltpu.SMEM),
            pl.BlockSpec((1, s, LANES), lambda i, h: (i, 0, COL_QA * 8 + h)),
            pl.BlockSpec((1, s, LANES), lambda i, h: (i, 0, COL_KA * 8 + h)),
            pl.BlockSpec((1, s, LANES), lambda i, h: (i, 0, COL_VA * 8 + h)),
            pl.BlockSpec((1, 2 * tk + LANES, LANES), lambda i, h: (h, 0, 0)),
            pl.BlockSpec((A_VDIM, LANES), lambda i, h: (0, 0)),
        ],
        out_specs=pl.BlockSpec((1, s, LANES), lambda i, h: (i, 0, h)),
        out_shape=jax.ShapeDtypeStruct((b, s, A_WIDTH), BF16),
        scratch_shapes=[pltpu.VMEM((s // tk, A_VDIM, tk), BF16)],
        compiler_params=pltpu.CompilerParams(
            dimension_semantics=("parallel", "arbitrary"), vmem_limit_bytes=VMEM_LIMIT),
        name="diffattn",
    )(scal, p3, p3, p3, strip, gcol)


def _cumsum_mm(tri, x):
    hi = x.astype(BF16)
    r1 = x - hi.astype(F32)
    mid = r1.astype(BF16)
    lo = (r1 - mid.astype(F32)).astype(BF16)
    d = lambda y: jnp.dot(tri, y, preferred_element_type=F32)
    return d(hi) + d(mid) + d(lo)


def _hgrn_chunk(q, z, v, lb, tri, mask, ref_row, last_row, st_ref):
    e = jnp.exp(-jnp.abs(z))
    rcp = 1.0 / (1.0 + e)
    s_big, s_small = rcp, e * rcp
    pos = z >= 0
    sig = jnp.where(pos, s_big, s_small)
    sig_neg = jnp.where(pos, s_small, s_big)
    oml = 1.0 - lb
    logf = jnp.log(lb + oml * sig)
    kk = oml * sig_neg
    b = _cumsum_mm(tri, logf)
    b_ref = b[ref_row:ref_row + 1, :]
    b_last = b[last_row:last_row + 1, :]
    qs = q * (B_DIM ** -0.5)
    qd = (qs * jnp.exp(b - b_ref)).astype(BF16)
    kd = (kk * jnp.exp(b_ref - b)).astype(BF16)
    kl = (kk * jnp.exp(b_last - b)).astype(BF16)
    qe = (qs * jnp.exp(b)).astype(BF16)
    dec = jnp.exp(b_last)
    vb = v.astype(BF16)
    outs = []
    for h in range(B_HEADS):
        sl = slice(h * B_DIM, (h + 1) * B_DIM)
        a = lax.dot_general(qd[:, sl], kd[:, sl], NT, preferred_element_type=F32)
        a = jnp.where(mask, a, 0.0).astype(BF16)
        intra = jnp.dot(a, vb[:, sl], preferred_element_type=F32)
        st = st_ref[h]
        inter = lax.dot_general(qe[:, sl], st.astype(BF16), NT, preferred_element_type=F32)
        outs.append(intra + inter)
        upd = lax.dot_general(vb[:, sl], kl[:, sl], TN, preferred_element_type=F32)
        st_ref[h] = st * dec[:, sl] + upd
    return jnp.concatenate(outs, axis=1)


def _hgrn_kernel(lb_ref, qf_ref, zf_ref, vf_ref, qb_ref, zb_ref, vb_ref, of_ref, ob_ref,
                 stf_ref, stb_ref, *, tc):
    @pl.when(pl.program_id(1) == 0)
    def _():
        stf_ref[...] = jnp.zeros_like(stf_ref)
        stb_ref[...] = jnp.zeros_like(stb_ref)

    nch = tc // CHUNK
    row = lax.broadcasted_iota(jnp.int32, (CHUNK, CHUNK), 0)
    col = lax.broadcasted_iota(jnp.int32, (CHUNK, CHUNK), 1)
    mask_f = col <= row
    mask_b = col >= row
    tri_f = mask_f.astype(BF16)
    tri_b = mask_b.astype(BF16)
    lb_f = lb_ref[0:1, :]
    lb_b = lb_ref[1:2, :]

    def body(c, carry):
        rf = pl.ds(pl.multiple_of(c * CHUNK, CHUNK), CHUNK)
        rb = pl.ds(pl.multiple_of((nch - 1 - c) * CHUNK, CHUNK), CHUNK)
        ld = lambda ref, rows: ref[0, rows, :].astype(F32)
        of_ref[0, rf, :] = _hgrn_chunk(ld(qf_ref, rf), ld(zf_ref, rf), ld(vf_ref, rf), lb_f, tri_f, mask_f,
                                       CHUNK // 2 - 1, CHUNK - 1, stf_ref).astype(of_ref.dtype)
        ob_ref[0, rb, :] = _hgrn_chunk(ld(qb_ref, rb), ld(zb_ref, rb), ld(vb_ref, rb), lb_b, tri_b, mask_b,
                                       CHUNK // 2, 0, stb_ref).astype(ob_ref.dtype)
        return carry

    lax.fori_loop(0, nch, body, 0)


def _hgrn(p3, lb2, tc):
    b, s, _ = p3.shape
    nb = s // tc
    fwd = lambda col: pl.BlockSpec((1, tc, B_WIDTH), lambda i, t: (i, t, col))
    bwd = lambda col: pl.BlockSpec((1, tc, B_WIDTH), lambda i, t: (i, nb - 1 - t, col))
    return pl.pallas_call(
        functools.partial(_hgrn_kernel, tc=tc),
        grid=(b, nb),
        in_specs=[
            pl.BlockSpec((2, B_WIDTH), lambda i, t: (0, 0)),
            fwd(COL_QB), fwd(COL_ZF), fwd(COL_VB),
            bwd(COL_QB), bwd(COL_ZB), bwd(COL_VB),
        ],
        out_specs=[
            pl.BlockSpec((1, tc, B_WIDTH), lambda i, t: (i, t, 0)),
            pl.BlockSpec((1, tc, B_WIDTH), lambda i, t: (i, nb - 1 - t, 0)),
        ],
        out_shape=[jax.ShapeDtypeStruct((b, s, B_WIDTH), BF16)] * 2,
        scratch_shapes=[pltpu.VMEM((B_HEADS, B_DIM, B_DIM), F32)] * 2,
        compiler_params=pltpu.CompilerParams(
            dimension_semantics=("parallel", "arbitrary"), vmem_limit_bytes=VMEM_LIMIT),
        name="hgrn2",
    )(lb2, p3, p3, p3, p3, p3, p3)


def _merge_kernel(x_ref, ya_ref, of_ref, ob_ref, gb_ref, gate_ref, kv_ref,
                  wa_ref, wb_ref, wo_ref, wq_ref, wox_ref,
                  ghg_ref, gpm_ref, gpx_ref, gpox_ref, o_ref):
    o = of_ref[...].astype(F32) + ob_ref[...].astype(F32)
    gb = gb_ref[...].astype(F32)
    ys = []
    for h in range(B_HEADS):
        sl = slice(h * B_DIM, (h + 1) * B_DIM)
        gh = gb[:, sl]
        ys.append((_rms(o[:, sl], ghg_ref[...]) * (gh * _sigmoid(gh))).astype(BF16))
    yb = jnp.concatenate(ys, axis=1)
    br_a = jnp.dot(ya_ref[...], wa_ref[...], preferred_element_type=F32)
    br_b = jnp.dot(yb, wb_ref[...], preferred_element_type=F32)
    g = gate_ref[...].astype(F32)
    mix = g[:, :D_MODEL] * br_a + g[:, D_MODEL:] * br_b
    t = jnp.dot(mix.astype(BF16), wo_ref[...], preferred_element_type=F32)
    x1 = x_ref[...] + _rms(t, gpm_ref[...])

    h2 = _rms(x1, gpx_ref[...]).astype(BF16)
    qx = (jnp.dot(h2, wq_ref[...], preferred_element_type=F32) * (X_DIM ** -0.5 * LOG2E)).astype(BF16)
    kv = kv_ref[0]
    oxs = []
    for h in range(X_HEADS):
        sl = slice(h * X_DIM, (h + 1) * X_DIM)
        lg = lax.dot_general(qx[:, sl], kv[:, sl], NT, preferred_element_type=F32)
        p = jnp.exp2(lg - jnp.max(lg, axis=-1, keepdims=True))
        l = jnp.sum(p, axis=-1, keepdims=True)
        vh = kv[:, X_WIDTH + h * X_DIM:X_WIDTH + (h + 1) * X_DIM]
        oxs.append((jnp.dot(p.astype(BF16), vh, preferred_element_type=F32) * (1.0 / l)).astype(BF16))
    ox = jnp.concatenate(oxs, axis=1)
    t2 = jnp.dot(ox, wox_ref[...], preferred_element_type=F32)
    o_ref[...] = x1 + _rms(t2, gpox_ref[...])


def _merge(x2d, ya2d, of2d, ob2d, p2d, kvx, wa, wb, wo, wq, wox, ghg, gpm, gpx, gpox, seq, tm):
    t = x2d.shape[0]
    per_seq = seq // tm
    const = lambda shape: pl.BlockSpec(shape, lambda i: (0,) * len(shape), pipeline_mode=pl.Buffered(1))
    return pl.pallas_call(
        _merge_kernel,
        grid=(t // tm,),
        in_specs=[
            pl.BlockSpec((tm, D_MODEL), lambda i: (i, 0)),
            pl.BlockSpec((tm, A_WIDTH), lambda i: (i, 0)),
            pl.BlockSpec((tm, B_WIDTH), lambda i: (i, 0)),
            pl.BlockSpec((tm, B_WIDTH), lambda i: (i, 0)),
            pl.BlockSpec((tm, B_WIDTH), lambda i: (i, COL_GB)),
            pl.BlockSpec((tm, 2 * D_MODEL), lambda i: (i, COL_GATE // 4)),
            pl.BlockSpec((1,) + kvx.shape[1:], lambda i: (i // per_seq, 0, 0)),
            const((A_WIDTH, D_MODEL)), const((B_WIDTH, D_MODEL)), const((D_MODEL, D_MODEL)),
            const((D_MODEL, X_WIDTH)), const((X_WIDTH, D_MODEL)),
            const((1, B_DIM)), const((1, D_MODEL)), const((1, D_MODEL)), const((1, D_MODEL)),
        ],
        out_specs=pl.BlockSpec((tm, D_MODEL), lambda i: (i, 0)),
        out_shape=jax.ShapeDtypeStruct((t, D_MODEL), F32),
        compiler_params=pltpu.CompilerParams(
            dimension_semantics=("parallel",), vmem_limit_bytes=VMEM_LIMIT),
        name="merge_xattn",
    )(x2d, ya2d, of2d, ob2d, p2d, p2d, kvx, wa, wb, wo, wq, wox, ghg, gpm, gpx, gpox)


def _ffn_kernel(x_ref, g_ref, wg_ref, wu_ref, wd_ref, gpost_ref, o_ref, h_ref, *, nf):
    f = pl.program_id(1)

    @pl.when(f == 0)
    def _():
        h_ref[...] = _rms(x_ref[...], g_ref[...]).astype(BF16)

    h = h_ref[...]
    gt = jnp.dot(h, wg_ref[...], preferred_element_type=F32)
    up = jnp.dot(h, wu_ref[...], preferred_element_type=F32)
    act = (gt * _sigmoid(gt) * up).astype(BF16)
    part = jnp.dot(act, wd_ref[...], preferred_element_type=F32)

    @pl.when(f == 0)
    def _():
        o_ref[...] = part

    @pl.when(jnp.logical_and(f > 0, f < nf - 1))
    def _():
        o_ref[...] += part

    @pl.when(f == nf - 1)
    def _():
        o_ref[...] = x_ref[...] + _rms(o_ref[...] + part, gpost_ref[...])


def _ffn(x2d, g, wgu, wd, gpost, tm, tf):
    t = x2d.shape[0]
    nf = D_FF // tf
    return pl.pallas_call(
        functools.partial(_ffn_kernel, nf=nf),
        grid=(t // tm, nf),
        in_specs=[
            pl.BlockSpec((tm, D_MODEL), lambda i, f: (i, 0)),
            pl.BlockSpec((1, D_MODEL), lambda i, f: (0, 0)),
            pl.BlockSpec((D_MODEL, tf), lambda i, f: (0, f)),
            pl.BlockSpec((D_MODEL, tf), lambda i, f: (0, nf + f)),
            pl.BlockSpec((tf, D_MODEL), lambda i, f: (f, 0)),
            pl.BlockSpec((1, D_MODEL), lambda i, f: (0, 0)),
        ],
        out_specs=pl.BlockSpec((tm, D_MODEL), lambda i, f: (i, 0)),
        out_shape=jax.ShapeDtypeStruct((t, D_MODEL), F32),
        scratch_shapes=[pltpu.VMEM((tm, D_MODEL), BF16)],
        compiler_params=pltpu.CompilerParams(
            dimension_semantics=("parallel", "arbitrary"), vmem_limit_bytes=VMEM_LIMIT),
        name="swiglu",
    )(x2d, g, wgu, wgu, wd, gpost)


def _rel_bucket(rel):
    nb = REL_BUCKETS // 2
    max_exact = nb // 2
    n = jnp.abs(rel)
    nf = jnp.maximum(n, 1).astype(jnp.float32)
    large = max_exact + (jnp.log(nf / max_exact) / math.log(REL_MAX_DIST / max_exact) * (nb - max_exact)).astype(jnp.int32)
    large = jnp.minimum(large, nb - 1)
    return jnp.where(rel > 0, nb, 0) + jnp.where(n < max_exact, n, large)


def _tiles(seq):
    tk = min(512, seq)
    tc = min(256, seq)
    tm_proj = min(1024, seq)
    tm_merge = min(256, seq)
    tm_ffn = min(512, seq)
    return tk, tc, tm_proj, tm_merge, tm_ffn


def _layer(x, mem, w, seq_tiles):
    b, s, _ = x.shape
    tk, tc, tm_proj, tm_merge, tm_ffn = seq_tiles
    x2d = x.reshape(b * s, D_MODEL)
    p2d = _inproj(x2d, w["g_pre_mix"], w["w_in"], w["colbias"], tm_proj)
    p3 = p2d.reshape(b, s, N_IN)
    ya = _attention(p3, w["attn_scal"], w["strip"][tk], w["g_subln_col"], tk)
    o_f, o_b = _hgrn(p3, w["lb2"], tc)
    kvx = _memkv(mem, w["g_mem"], w["w_kv_x"])
    x2 = _merge(x2d, ya.reshape(b * s, A_WIDTH), o_f.reshape(b * s, B_WIDTH), o_b.reshape(b * s, B_WIDTH),
                p2d, kvx, w["w_branch_a"], w["w_branch_b"], w["w_out"], w["w_q_x"], w["w_o_x"],
                w["g_hgrn_norm"], w["g_post_mix"], w["g_pre_x"], w["g_post_x"], s, tm_merge)
    y = _ffn(x2, w["g_pre_ffn"], w["w_gate_up"], w["w_down"], w["g_post_ffn"], tm_ffn, 512)
    return y.reshape(b, s, D_MODEL)


def _bias_strip(rel_bias, tk):
    r = jnp.arange(-tk, tk + LANES, dtype=jnp.int32)[:, None]
    i = jnp.arange(LANES, dtype=jnp.int32)[None, :]
    tbl = rel_bias[_rel_bucket(r - i)].astype(F32) * LOG2E
    return jnp.transpose(tbl, (2, 0, 1))


def kernel(x_prompt, x_sample, mem_prompt, mem_sample, rel_bias, hgrn_lb_logits, g_pre_mix, w_in, b_merge,
           lam_q1, lam_k1, lam_q2, lam_k2, g_subln, g_hgrn_norm, w_branch_a, w_branch_b, w_out, g_post_mix,
           g_pre_x, g_mem, w_q_x, w_kv_x, w_o_x, g_post_x, g_pre_ffn, w_gate_up, w_down, g_post_ffn):
    l = 0
    row = lambda a: a[l].astype(F32).reshape(1, -1)
    lam = (jnp.exp(jnp.sum(lam_q1[l].astype(F32) * lam_k1[l].astype(F32)))
           - jnp.exp(jnp.sum(lam_q2[l].astype(F32) * lam_k2[l].astype(F32))) + LAM_INIT)
    lb_all = jnp.cumsum(jax.nn.softmax(hgrn_lb_logits.astype(F32), axis=1), axis=1)
    nb = REL_BUCKETS // 2
    ones = jnp.ones((A_HEADS,), F32)
    attn_scal = jnp.stack([rel_bias[nb - 1].astype(F32) * LOG2E, rel_bias[2 * nb - 1].astype(F32) * LOG2E,
                           lam * ones, (1.0 - LAM_INIT) * ones], axis=1)
    tiles_p = _tiles(x_prompt.shape[1])
    tiles_s = _tiles(x_sample.shape[1])
    w = dict(
        g_pre_mix=row(g_pre_mix), w_in=w_in[l].astype(BF16),
        colbias=jnp.concatenate([jnp.zeros((1, N_IN - 2 * D_MODEL), F32), row(b_merge)], axis=1),
        attn_scal=attn_scal,
        strip={tk: _bias_strip(rel_bias, tk) for tk in {tiles_p[0], tiles_s[0]}},
        g_subln_col=jnp.broadcast_to(g_subln[l].astype(F32)[:, None], (A_VDIM, LANES)),
        lb2=jnp.stack([lb_all[0, l], lb_all[1, l]], axis=0),
        g_mem=row(g_mem), w_kv_x=w_kv_x[l].astype(BF16),
        w_branch_a=w_branch_a[l].astype(BF16), w_branch_b=w_branch_b[l].astype(BF16),
        w_out=w_out[l].astype(BF16), w_q_x=w_q_x[l].astype(BF16), w_o_x=w_o_x[l].astype(BF16),
        g_hgrn_norm=row(g_hgrn_norm), g_post_mix=row(g_post_mix), g_pre_x=row(g_pre_x), g_post_x=row(g_post_x),
        g_pre_ffn=row(g_pre_ffn), w_gate_up=w_gate_up[l].astype(BF16), w_down=w_down[l].astype(BF16),
        g_post_ffn=row(g_post_ffn),
    )
    return (_layer(x_prompt, mem_prompt, w, tiles_p), _layer(x_sample, mem_sample, w, tiles_s))
```

```python
import functools
import math

import jax
import jax.numpy as jnp
import numpy as np
from jax import lax
from jax.experimental import pallas as pl
from jax.experimental.pallas import tpu as pltpu

F32 = jnp.float32
BF16 = jnp.bfloat16

D_MODEL = 2048
A_WIDTH = 1024
A_HALF = 64
A_HEADS = 8
A_VDIM = 128
B_WIDTH = 1024
B_DIM = 128
B_HEADS = 8
CHUNK = 64
REL_BUCKETS = 32
REL_MAX_DIST = 128
X_HEADS = 4
X_DIM = 128
X_WIDTH = 512
D_FF = 5632
N_IN = 12288
EPS = 1e-6
LOG2E = 1.4426950408889634
LAM_INIT = 0.8 - 0.6 * math.exp(-0.3 * 0)

LANES = 128
VMEM_LIMIT = 56 * 1024 * 1024

COL_QA, COL_KA, COL_VA, COL_QB, COL_ZF, COL_ZB, COL_VB, COL_GB, COL_GATE = range(9)

NT = (((1,), (1,)), ((), ()))
TN = (((0,), (0,)), ((), ()))


def _rms(x, g):
    ms = jnp.mean(x * x, axis=-1, keepdims=True)
    return x * lax.rsqrt(ms + EPS) * g


def _sigmoid(x):
    return 1.0 / (1.0 + jnp.exp(-x))


def _inproj_kernel(x_ref, g_ref, w_ref, cb_ref, o_ref, h_ref):
    j = pl.program_id(1)

    @pl.when(j == 0)
    def _():
        h_ref[...] = _rms(x_ref[...], g_ref[...]).astype(BF16)

    acc = jnp.dot(h_ref[...], w_ref[...], preferred_element_type=F32)

    @pl.when(j == COL_QA)
    def _():
        o_ref[...] = (acc * (A_HALF ** -0.5 * LOG2E)).astype(BF16)

    @pl.when(jnp.logical_and(j > COL_QA, j < COL_GATE))
    def _():
        o_ref[...] = acc.astype(BF16)

    @pl.when(j >= COL_GATE)
    def _():
        o_ref[...] = _sigmoid(acc + cb_ref[...]).astype(BF16)


def _inproj(x2d, g, w, colbias, tm):
    t = x2d.shape[0]
    tn = 1024
    return pl.pallas_call(
        _inproj_kernel,
        grid=(t // tm, N_IN // tn),
        in_specs=[
            pl.BlockSpec((tm, D_MODEL), lambda i, j: (i, 0)),
            pl.BlockSpec((1, D_MODEL), lambda i, j: (0, 0)),
            pl.BlockSpec((D_MODEL, tn), lambda i, j: (0, j)),
            pl.BlockSpec((1, tn), lambda i, j: (0, j)),
        ],
        out_specs=pl.BlockSpec((tm, tn), lambda i, j: (i, j)),
        out_shape=jax.ShapeDtypeStruct((t, N_IN), BF16),
        scratch_shapes=[pltpu.VMEM((tm, D_MODEL), BF16)],
        compiler_params=pltpu.CompilerParams(
            dimension_semantics=("parallel", "arbitrary"), vmem_limit_bytes=VMEM_LIMIT),
        name="inproj",
    )(x2d, g, w, colbias)


def _memkv_kernel(m_ref, g_ref, w_ref, o_ref):
    h = _rms(m_ref[0], g_ref[...]).astype(BF16)
    o_ref[0] = jnp.dot(h, w_ref[...], preferred_element_type=F32).astype(BF16)


def _memkv(mem, g, w):
    b, n, _ = mem.shape
    return pl.pallas_call(
        _memkv_kernel,
        grid=(b,),
        in_specs=[
            pl.BlockSpec((1, n, D_MODEL), lambda i: (i, 0, 0)),
            pl.BlockSpec((1, D_MODEL), lambda i: (0, 0)),
            pl.BlockSpec((D_MODEL, 2 * X_WIDTH), lambda i: (0, 0)),
        ],
        out_specs=pl.BlockSpec((1, n, 2 * X_WIDTH), lambda i: (i, 0, 0)),
        out_shape=jax.ShapeDtypeStruct((b, n, 2 * X_WIDTH), BF16),
        compiler_params=pltpu.CompilerParams(
            dimension_semantics=("parallel",), vmem_limit_bytes=VMEM_LIMIT),
        name="memkv",
    )(mem, g, w)


def _attn_kernel(sc_ref, q_ref, k_ref, v_ref, strip_ref, g_ref, o_ref, vt_ref, sa_ref, sb_ref, acc_ref,
                 *, seq, tk, ns):
    h = pl.program_id(1)
    tq = ns * LANES
    nk = seq // tk
    nt = (seq // tq) * nk
    r = tk // LANES
    lam = sc_ref[h, 0]
    post = sc_ref[h, 1]

    row = lax.broadcasted_iota(jnp.int32, (LANES, LANES), 0)
    lane = lax.broadcasted_iota(jnp.int32, (LANES, LANES), 1)
    eye = (row == lane).astype(BF16)

    def vt_body(j, carry):
        vc = v_ref[0, pl.ds(pl.multiple_of(j * tk, tk), tk), :]
        vt_ref[j] = lax.dot_general(eye, vc, NT, preferred_element_type=F32).astype(BF16)
        return carry

    lax.fori_loop(0, nk, vt_body, 0)

    def logits(t, dst_ref):
        qi = t // nk
        j = t - qi * nk
        kc = k_ref[0, pl.ds(pl.multiple_of(j * tk, tk), tk), :]
        r0 = jnp.clip((j * r - qi * ns) * LANES, -2 * tk, tk + tq)
        bias = strip_ref[0, pl.ds(pl.multiple_of(r0 + 2 * tk, LANES), tk), :]
        maxima = []
        for s in range(ns):
            qs = q_ref[0, pl.ds(pl.multiple_of(qi * tq + s * LANES, LANES), LANES), :]
            zero = jnp.zeros_like(qs)
            qst = jnp.concatenate([jnp.where(lane < A_HALF, qs, zero),
                                   jnp.where(lane >= A_HALF, qs, zero)], axis=0)
            b = bias[:, s * LANES:(s + 1) * LANES]
            st = (lax.dot_general(kc, qst, NT, preferred_element_type=F32)
                  + jnp.concatenate([b, b], axis=1))
            dst_ref[s] = st
            maxima.append(jnp.max(st, axis=0, keepdims=True))
        return tuple(maxima)

    def step(t, cur_ref, nxt_ref, carry, may_finish):
        qi = t // nk
        j = t - qi * nk
        nxt_max = logits(jnp.minimum(t + 1, nt - 1), nxt_ref)
        vt = vt_ref[j]
        first = j == 0
        new = []
        for s in range(ns):
            m, l, cur_max = carry[s]
            m = jnp.where(first, -1e30, m)
            m_new = jnp.maximum(m, cur_max)
            alpha = jnp.exp2(m - m_new)
            p = jnp.exp2(cur_ref[s] - m_new)
            l = alpha * l + jnp.sum(p, axis=0, keepdims=True)
            pv = jnp.dot(vt, p.astype(BF16), preferred_element_type=F32)
            acc_ref[s] = alpha * acc_ref[s] + pv
            new.append((m_new, l, nxt_max[s]))

        if may_finish:
            @pl.when(j == nk - 1)
            def _():
                for s in range(ns):
                    o = acc_ref[s] * (1.0 / new[s][1])
                    ot = o[:, :LANES] - lam * o[:, LANES:]
                    ms = jnp.mean(ot * ot, axis=0, keepdims=True)
                    y = ot * lax.rsqrt(ms + EPS) * g_ref[...] * post
                    o_ref[0, pl.ds(pl.multiple_of(qi * tq + s * LANES, LANES), LANES), :] = y.T.astype(BF16)

        return tuple(new)

    def body(u, carry):
        carry = step(2 * u, sa_ref, sb_ref, carry, may_finish=nk % 2 == 1)
        return step(2 * u + 1, sb_ref, sa_ref, carry, may_finish=True)

    acc_ref[...] = jnp.zeros_like(acc_ref)
    max0 = logits(0, sa_ref)
    init = tuple((jnp.full((1, 2 * LANES), -1e30, F32), jnp.zeros((1, 2 * LANES), F32), max0[s])
                 for s in range(ns))
    lax.fori_loop(0, nt // 2, body, init)


def _attention(p3, scal, strip, gcol, tk, ns):
    b, s, _ = p3.shape
    kern = functools.partial(_attn_kernel, seq=s, tk=tk, ns=ns)
    return pl.pallas_call(
        kern,
        grid=(b, A_HEADS),
        in_specs=[
            pl.BlockSpec(memory_space=pltpu.SMEM),
            pl.BlockSpec((1, s, LANES), lambda i, h: (i, 0, COL_QA * 8 + h)),
            pl.BlockSpec((1, s, LANES), lambda i, h: (i, 0, COL_KA * 8 + h)),
            pl.BlockSpec((1, s, LANES), lambda i, h: (i, 0, COL_VA * 8 + h)),
            pl.BlockSpec((1, 4 * tk + ns * LANES, ns * LANES), lambda i, h: (h, 0, 0)),
            pl.BlockSpec((A_VDIM, LANES), lambda i, h: (0, 0)),
        ],
        out_specs=pl.BlockSpec((1, s, LANES), lambda i, h: (i, 0, h)),
        out_shape=jax.ShapeDtypeStruct((b, s, A_WIDTH), BF16),
        scratch_shapes=[pltpu.VMEM((s // tk, A_VDIM, tk), BF16),
                        pltpu.VMEM((ns, tk, 2 * LANES), F32),
                        pltpu.VMEM((ns, tk, 2 * LANES), F32),
                        pltpu.VMEM((ns, A_VDIM, 2 * LANES), F32)],
        compiler_params=pltpu.CompilerParams(
            dimension_semantics=("parallel", "arbitrary"), vmem_limit_bytes=VMEM_LIMIT),
        name="diffattn",
    )(scal, p3, p3, p3, strip, gcol)


def _cumsum_mm(tri, x):
    hi = x.astype(BF16)
    r1 = x - hi.astype(F32)
    mid = r1.astype(BF16)
    lo = (r1 - mid.astype(F32)).astype(BF16)
    d = lambda y: jnp.dot(tri, y, preferred_element_type=F32)
    return d(hi) + d(mid) + d(lo)


def _hgrn_chunk(q, z, v, lb, tri, mask, ref_row, last_row, st_ref):
    e = jnp.exp(-jnp.abs(z))
    rcp = 1.0 / (1.0 + e)
    s_big, s_small = rcp, e * rcp
    pos = z >= 0
    sig = jnp.where(pos, s_big, s_small)
    sig_neg = jnp.where(pos, s_small, s_big)
    oml = 1.0 - lb
    logf = jnp.log(lb + oml * sig)
    kk = oml * sig_neg
    b = _cumsum_mm(tri, logf)
    b_ref = b[ref_row:ref_row + 1, :]
    b_last = b[last_row:last_row + 1, :]
    qs = q * (B_DIM ** -0.5)
    qd = (qs * jnp.exp(b - b_ref)).astype(BF16)
    kd = (kk * jnp.exp(b_ref - b)).astype(BF16)
    kl = (kk * jnp.exp(b_last - b)).astype(BF16)
    qe = (qs * jnp.exp(b)).astype(BF16)
    dec = jnp.exp(b_last)
    vb = v.astype(BF16)
    outs = []
    for h in range(B_HEADS):
        sl = slice(h * B_DIM, (h + 1) * B_DIM)
        a = lax.dot_general(qd[:, sl], kd[:, sl], NT, preferred_element_type=F32)
        a = jnp.where(mask, a, 0.0).astype(BF16)
        intra = jnp.dot(a, vb[:, sl], preferred_element_type=F32)
        st = st_ref[h]
        inter = lax.dot_general(qe[:, sl], st.astype(BF16), NT, preferred_element_type=F32)
        outs.append(intra + inter)
        upd = lax.dot_general(vb[:, sl], kl[:, sl], TN, preferred_element_type=F32)
        st_ref[h] = st * dec[:, sl] + upd
    return jnp.concatenate(outs, axis=1)


def _hgrn_kernel(lb_ref, qf_ref, zf_ref, vf_ref, qb_ref, zb_ref, vb_ref, of_ref, ob_ref,
                 stf_ref, stb_ref, *, tc):
    @pl.when(pl.program_id(1) == 0)
    def _():
        stf_ref[...] = jnp.zeros_like(stf_ref)
        stb_ref[...] = jnp.zeros_like(stb_ref)

    nch = tc // CHUNK
    row = lax.broadcasted_iota(jnp.int32, (CHUNK, CHUNK), 0)
    col = lax.broadcasted_iota(jnp.int32, (CHUNK, CHUNK), 1)
    mask_f = col <= row
    mask_b = col >= row
    tri_f = mask_f.astype(BF16)
    tri_b = mask_b.astype(BF16)
    lb_f = lb_ref[0:1, :]
    lb_b = lb_ref[1:2, :]

    def body(c, carry):
        rf = pl.ds(pl.multiple_of(c * CHUNK, CHUNK), CHUNK)
        rb = pl.ds(pl.multiple_of((nch - 1 - c) * CHUNK, CHUNK), CHUNK)
        ld = lambda ref, rows: ref[0, rows, :].astype(F32)
        of_ref[0, rf, :] = _hgrn_chunk(ld(qf_ref, rf), ld(zf_ref, rf), ld(vf_ref, rf), lb_f, tri_f, mask_f,
                                       CHUNK // 2 - 1, CHUNK - 1, stf_ref).astype(of_ref.dtype)
        ob_ref[0, rb, :] = _hgrn_chunk(ld(qb_ref, rb), ld(zb_ref, rb), ld(vb_ref, rb), lb_b, tri_b, mask_b,
                                       CHUNK // 2, 0, stb_ref).astype(ob_ref.dtype)
        return carry

    lax.fori_loop(0, nch, body, 0)


def _hgrn(p3, lb2, tc):
    b, s, _ = p3.shape
    nb = s // tc
    fwd = lambda col: pl.BlockSpec((1, tc, B_WIDTH), lambda i, t: (i, t, col))
    bwd = lambda col: pl.BlockSpec((1, tc, B_WIDTH), lambda i, t: (i, nb - 1 - t, col))
    return pl.pallas_call(
        functools.partial(_hgrn_kernel, tc=tc),
        grid=(b, nb),
        in_specs=[
            pl.BlockSpec((2, B_WIDTH), lambda i, t: (0, 0)),
            fwd(COL_QB), fwd(COL_ZF), fwd(COL_VB),
            bwd(COL_QB), bwd(COL_ZB), bwd(COL_VB),
        ],
        out_specs=[
            pl.BlockSpec((1, tc, B_WIDTH), lambda i, t: (i, t, 0)),
            pl.BlockSpec((1, tc, B_WIDTH), lambda i, t: (i, nb - 1 - t, 0)),
        ],
        out_shape=[jax.ShapeDtypeStruct((b, s, B_WIDTH), BF16)] * 2,
        scratch_shapes=[pltpu.VMEM((B_HEADS, B_DIM, B_DIM), F32)] * 2,
        compiler_params=pltpu.CompilerParams(
            dimension_semantics=("parallel", "arbitrary"), vmem_limit_bytes=VMEM_LIMIT),
        name="hgrn2",
    )(lb2, p3, p3, p3, p3, p3, p3)


def _merge_kernel(x_ref, ya_ref, of_ref, ob_ref, gb_ref, gate_ref, kv_ref,
                  wa_ref, wb_ref, wo_ref, wq_ref, wox_ref,
                  ghg_ref, gpm_ref, gpx_ref, gpox_ref, o_ref):
    o = of_ref[...].astype(F32) + ob_ref[...].astype(F32)
    gb = gb_ref[...].astype(F32)
    ys = []
    for h in range(B_HEADS):
        sl = slice(h * B_DIM, (h + 1) * B_DIM)
        gh = gb[:, sl]
        ys.append((_rms(o[:, sl], ghg_ref[...]) * (gh * _sigmoid(gh))).astype(BF16))
    yb = jnp.concatenate(ys, axis=1)
    br_a = jnp.dot(ya_ref[...], wa_ref[...], preferred_element_type=F32)
    br_b = jnp.dot(yb, wb_ref[...], preferred_element_type=F32)
    g = gate_ref[...].astype(F32)
    mix = g[:, :D_MODEL] * br_a + g[:, D_MODEL:] * br_b
    t = jnp.dot(mix.astype(BF16), wo_ref[...], preferred_element_type=F32)
    x1 = x_ref[...] + _rms(t, gpm_ref[...])

    h2 = _rms(x1, gpx_ref[...]).astype(BF16)
    qx = (jnp.dot(h2, wq_ref[...], preferred_element_type=F32) * (X_DIM ** -0.5 * LOG2E)).astype(BF16)
    kv = kv_ref[0]
    oxs = []
    for h in range(X_HEADS):
        sl = slice(h * X_DIM, (h + 1) * X_DIM)
        lg = lax.dot_general(qx[:, sl], kv[:, sl], NT, preferred_element_type=F32)
        p = jnp.exp2(lg - jnp.max(lg, axis=-1, keepdims=True))
        l = jnp.sum(p, axis=-1, keepdims=True)
        vh = kv[:, X_WIDTH + h * X_DIM:X_WIDTH + (h + 1) * X_DIM]
        oxs.append((jnp.dot(p.astype(BF16), vh, preferred_element_type=F32) * (1.0 / l)).astype(BF16))
    ox = jnp.concatenate(oxs, axis=1)
    t2 = jnp.dot(ox, wox_ref[...], preferred_element_type=F32)
    o_ref[...] = x1 + _rms(t2, gpox_ref[...])


def _merge(x2d, ya2d, of2d, ob2d, p2d, kvx, wa, wb, wo, wq, wox, ghg, gpm, gpx, gpox, seq, tm):
    t = x2d.shape[0]
    per_seq = seq // tm
    const = lambda shape: pl.BlockSpec(shape, lambda i: (0,) * len(shape), pipeline_mode=pl.Buffered(1))
    return pl.pallas_call(
        _merge_kernel,
        grid=(t // tm,),
        in_specs=[
            pl.BlockSpec((tm, D_MODEL), lambda i: (i, 0)),
            pl.BlockSpec((tm, A_WIDTH), lambda i: (i, 0)),
            pl.BlockSpec((tm, B_WIDTH), lambda i: (i, 0)),
            pl.BlockSpec((tm, B_WIDTH), lambda i: (i, 0)),
            pl.BlockSpec((tm, B_WIDTH), lambda i: (i, COL_GB)),
            pl.BlockSpec((tm, 2 * D_MODEL), lambda i: (i, COL_GATE // 4)),
            pl.BlockSpec((1,) + kvx.shape[1:], lambda i: (i // per_seq, 0, 0)),
            const((A_WIDTH, D_MODEL)), const((B_WIDTH, D_MODEL)), const((D_MODEL, D_MODEL)),
            const((D_MODEL, X_WIDTH)), const((X_WIDTH, D_MODEL)),
            const((1, B_DIM)), const((1, D_MODEL)), const((1, D_MODEL)), const((1, D_MODEL)),
        ],
        out_specs=pl.BlockSpec((tm, D_MODEL), lambda i: (i, 0)),
        out_shape=jax.ShapeDtypeStruct((t, D_MODEL), F32),
        compiler_params=pltpu.CompilerParams(
            dimension_semantics=("parallel",), vmem_limit_bytes=VMEM_LIMIT),
        name="merge_xattn",
    )(x2d, ya2d, of2d, ob2d, p2d, p2d, kvx, wa, wb, wo, wq, wox, ghg, gpm, gpx, gpox)


def _ffn_kernel(x_ref, g_ref, wg_ref, wu_ref, wd_ref, gpost_ref, o_ref, h_ref, *, nf):
    f = pl.program_id(1)

    @pl.when(f == 0)
    def _():
        h_ref[...] = _rms(x_ref[...], g_ref[...]).astype(BF16)

    h = h_ref[...]
    gt = jnp.dot(h, wg_ref[...], preferred_element_type=F32)
    up = jnp.dot(h, wu_ref[...], preferred_element_type=F32)
    act = (gt * _sigmoid(gt) * up).astype(BF16)
    part = jnp.dot(act, wd_ref[...], preferred_element_type=F32)

    @pl.when(f == 0)
    def _():
        o_ref[...] = part

    @pl.when(jnp.logical_and(f > 0, f < nf - 1))
    def _():
        o_ref[...] += part

    @pl.when(f == nf - 1)
    def _():
        o_ref[...] = x_ref[...] + _rms(o_ref[...] + part, gpost_ref[...])


def _ffn(x2d, g, wgu, wd, gpost, tm, tf):
    t = x2d.shape[0]
    nf = D_FF // tf
    return pl.pallas_call(
        functools.partial(_ffn_kernel, nf=nf),
        grid=(t // tm, nf),
        in_specs=[
            pl.BlockSpec((tm, D_MODEL), lambda i, f: (i, 0)),
            pl.BlockSpec((1, D_MODEL), lambda i, f: (0, 0)),
            pl.BlockSpec((D_MODEL, tf), lambda i, f: (0, f)),
            pl.BlockSpec((D_MODEL, tf), lambda i, f: (0, nf + f)),
            pl.BlockSpec((tf, D_MODEL), lambda i, f: (f, 0)),
            pl.BlockSpec((1, D_MODEL), lambda i, f: (0, 0)),
        ],
        out_specs=pl.BlockSpec((tm, D_MODEL), lambda i, f: (i, 0)),
        out_shape=jax.ShapeDtypeStruct((t, D_MODEL), F32),
        scratch_shapes=[pltpu.VMEM((tm, D_MODEL), BF16)],
        compiler_params=pltpu.CompilerParams(
            dimension_semantics=("parallel", "arbitrary"), vmem_limit_bytes=VMEM_LIMIT),
        name="swiglu",
    )(x2d, g, wgu, wgu, wd, gpost)


def _rel_bucket(rel):
    nb = REL_BUCKETS // 2
    max_exact = nb // 2
    n = jnp.abs(rel)
    nf = jnp.maximum(n, 1).astype(jnp.float32)
    large = max_exact + (jnp.log(nf / max_exact) / math.log(REL_MAX_DIST / max_exact) * (nb - max_exact)).astype(jnp.int32)
    large = jnp.minimum(large, nb - 1)
    return jnp.where(rel > 0, nb, 0) + jnp.where(n < max_exact, n, large)


def _tiles(seq):
    tk = min(512, seq)
    ns = 2
    tc = min(256, seq)
    tm_proj = min(1024, seq)
    tm_merge = min(256, seq)
    tm_ffn = min(512, seq)
    return tk, ns, tc, tm_proj, tm_merge, tm_ffn


def _layer(x, mem, w, seq_tiles):
    b, s, _ = x.shape
    tk, ns, tc, tm_proj, tm_merge, tm_ffn = seq_tiles
    x2d = x.reshape(b * s, D_MODEL)
    p2d = _inproj(x2d, w["g_pre_mix"], w["w_in"], w["colbias"], tm_proj)
    p3 = p2d.reshape(b, s, N_IN)
    ya = _attention(p3, w["attn_scal"], w["strip"][(tk, ns)], w["g_subln_col"], tk, ns)
    o_f, o_b = _hgrn(p3, w["lb2"], tc)
    kvx = _memkv(mem, w["g_mem"], w["w_kv_x"])
    x2 = _merge(x2d, ya.reshape(b * s, A_WIDTH), o_f.reshape(b * s, B_WIDTH), o_b.reshape(b * s, B_WIDTH),
                p2d, kvx, w["w_branch_a"], w["w_branch_b"], w["w_out"], w["w_q_x"], w["w_o_x"],
                w["g_hgrn_norm"], w["g_post_mix"], w["g_pre_x"], w["g_post_x"], s, tm_merge)
    y = _ffn(x2, w["g_pre_ffn"], w["w_gate_up"], w["w_down"], w["g_post_ffn"], tm_ffn, 512)
    return y.reshape(b, s, D_MODEL)


def _bias_strip(rel_bias, tk, tq):
    r = jnp.arange(-2 * tk, 2 * tk + tq, dtype=jnp.int32)[:, None]
    i = jnp.arange(tq, dtype=jnp.int32)[None, :]
    onehot = (_rel_bucket(r - i)[:, :, None] == jnp.arange(REL_BUCKETS, dtype=jnp.int32)).astype(F32)
    return jnp.einsum("rik,kh->hri", onehot, rel_bias.astype(F32) * LOG2E, precision=lax.Precision.HIGHEST)


def kernel(x_prompt, x_sample, mem_prompt, mem_sample, rel_bias, hgrn_lb_logits, g_pre_mix, w_in, b_merge,
           lam_q1, lam_k1, lam_q2, lam_k2, g_subln, g_hgrn_norm, w_branch_a, w_branch_b, w_out, g_post_mix,
           g_pre_x, g_mem, w_q_x, w_kv_x, w_o_x, g_post_x, g_pre_ffn, w_gate_up, w_down, g_post_ffn):
    l = 0
    row = lambda a: a[l].astype(F32).reshape(1, -1)
    lam = (jnp.exp(jnp.sum(lam_q1[l].astype(F32) * lam_k1[l].astype(F32)))
           - jnp.exp(jnp.sum(lam_q2[l].astype(F32) * lam_k2[l].astype(F32))) + LAM_INIT)
    lb_all = jnp.cumsum(jax.nn.softmax(hgrn_lb_logits.astype(F32), axis=1), axis=1)
    ones = jnp.ones((A_HEADS,), F32)
    attn_scal = jnp.stack([lam * ones, (1.0 - LAM_INIT) * ones], axis=1)
    tiles_p = _tiles(x_prompt.shape[1])
    tiles_s = _tiles(x_sample.shape[1])
    w = dict(
        g_pre_mix=row(g_pre_mix), w_in=w_in[l].astype(BF16),
        colbias=jnp.concatenate([jnp.zeros((1, N_IN - 2 * D_MODEL), F32), row(b_merge)], axis=1),
        attn_scal=attn_scal,
        strip={(tk, ns): _bias_strip(rel_bias, tk, ns * LANES) for tk, ns in {tiles_p[:2], tiles_s[:2]}},
        g_subln_col=jnp.broadcast_to(g_subln[l].astype(F32)[:, None], (A_VDIM, LANES)),
        lb2=jnp.stack([lb_all[0, l], lb_all[1, l]], axis=0),
        g_mem=row(g_mem), w_kv_x=w_kv_x[l].astype(BF16),
        w_branch_a=w_branch_a[l].astype(BF16), w_branch_b=w_branch_b[l].astype(BF16),
        w_out=w_out[l].astype(BF16), w_q_x=w_q_x[l].astype(BF16), w_o_x=w_o_x[l].astype(BF16),
        g_hgrn_norm=row(g_hgrn_norm), g_post_mix=row(g_post_mix), g_pre_x=row(g_pre_x), g_post_x=row(g_post_x),
        g_pre_ffn=row(g_pre_ffn), w_gate_up=w_gate_up[l].astype(BF16), w_down=w_down[l].astype(BF16),
        g_post_ffn=row(g_post_ffn),
    )
    return (_layer(x_prompt, mem_prompt, w, tiles_p), _layer(x_sample, mem_sample, w, tiles_s))
```

```python
import functools
import math

import jax
import jax.numpy as jnp
import numpy as np
from jax import lax
from jax.experimental import pallas as pl
from jax.experimental.pallas import tpu as pltpu

F32 = jnp.float32
BF16 = jnp.bfloat16

D_MODEL = 2048
A_WIDTH = 1024
A_HALF = 64
A_HEADS = 8
A_VDIM = 128
B_WIDTH = 1024
B_DIM = 128
B_HEADS = 8
CHUNK = 64
REL_BUCKETS = 32
REL_MAX_DIST = 128
X_HEADS = 4
X_DIM = 128
X_WIDTH = 512
D_FF = 5632
N_IN = 12288
EPS = 1e-6
LOG2E = 1.4426950408889634
LAM_INIT = 0.8 - 0.6 * math.exp(-0.3 * 0)

LANES = 128
VT_ROWS = A_VDIM + 16
ATTN_BUFS = 2
VMEM_LIMIT = 56 * 1024 * 1024

COL_QA, COL_KA, COL_VA, COL_QB, COL_ZF, COL_ZB, COL_VB, COL_GB, COL_GATE = range(9)

NT = (((1,), (1,)), ((), ()))
TN = (((0,), (0,)), ((), ()))


def _rms(x, g):
    ms = jnp.mean(x * x, axis=-1, keepdims=True)
    return x * lax.rsqrt(ms + EPS) * g


def _sigmoid(x):
    return 1.0 / (1.0 + jnp.exp(-x))


def _colmax(x):
    while x.shape[0] > 64:
        x = jnp.max(x.reshape(8, x.shape[0] // 8, x.shape[1]), axis=0)
    return jnp.max(x, axis=0, keepdims=True)


def _inproj_kernel(x_ref, g_ref, w_ref, cb_ref, o_ref, h_ref):
    j = pl.program_id(1)

    @pl.when(j == 0)
    def _():
        h_ref[...] = _rms(x_ref[...], g_ref[...]).astype(BF16)

    acc = jnp.dot(h_ref[...], w_ref[...], preferred_element_type=F32)

    @pl.when(j == COL_QA)
    def _():
        o_ref[...] = (acc * (A_HALF ** -0.5 * LOG2E)).astype(BF16)

    @pl.when(j == COL_QB)
    def _():
        o_ref[...] = (acc * B_DIM ** -0.5).astype(BF16)

    @pl.when(jnp.logical_and(jnp.logical_and(j > COL_QA, j < COL_GATE), j != COL_QB))
    def _():
        o_ref[...] = acc.astype(BF16)

    @pl.when(j >= COL_GATE)
    def _():
        o_ref[...] = _sigmoid(acc + cb_ref[...]).astype(BF16)


def _inproj(x2d, g, w, colbias, tm):
    t = x2d.shape[0]
    tn = 1024
    return pl.pallas_call(
        _inproj_kernel,
        grid=(t // tm, N_IN // tn),
        in_specs=[
            pl.BlockSpec((tm, D_MODEL), lambda i, j: (i, 0)),
            pl.BlockSpec((1, D_MODEL), lambda i, j: (0, 0)),
            pl.BlockSpec((D_MODEL, tn), lambda i, j: (0, j)),
            pl.BlockSpec((1, tn), lambda i, j: (0, j)),
        ],
        out_specs=pl.BlockSpec((tm, tn), lambda i, j: (i, j)),
        out_shape=jax.ShapeDtypeStruct((t, N_IN), BF16),
        scratch_shapes=[pltpu.VMEM((tm, D_MODEL), BF16)],
        compiler_params=pltpu.CompilerParams(
            dimension_semantics=("parallel", "arbitrary"), vmem_limit_bytes=VMEM_LIMIT),
        name="inproj",
    )(x2d, g, w, colbias)


def _memkv_kernel(m_ref, g_ref, w_ref, o_ref):
    h = _rms(m_ref[0], g_ref[...]).astype(BF16)
    o_ref[0] = jnp.dot(h, w_ref[...], preferred_element_type=F32).astype(BF16)


def _memkv(mem, g, w):
    b, n, _ = mem.shape
    return pl.pallas_call(
        _memkv_kernel,
        grid=(b,),
        in_specs=[
            pl.BlockSpec((1, n, D_MODEL), lambda i: (i, 0, 0)),
            pl.BlockSpec((1, D_MODEL), lambda i: (0, 0)),
            pl.BlockSpec((D_MODEL, 2 * X_WIDTH), lambda i: (0, 0)),
        ],
        out_specs=pl.BlockSpec((1, n, 2 * X_WIDTH), lambda i: (i, 0, 0)),
        out_shape=jax.ShapeDtypeStruct((b, n, 2 * X_WIDTH), BF16),
        compiler_params=pltpu.CompilerParams(
            dimension_semantics=("parallel",), vmem_limit_bytes=VMEM_LIMIT),
        name="memkv",
    )(mem, g, w)


def _attn_kernel(sc_ref, q_ref, k_ref, v_ref, strip_ref, g_ref, o_ref, vt_ref, *scratch, seq, tk, ns):
    bufs, qst_ref, acc_ref = scratch[:-2], scratch[-2], scratch[-1]
    h = pl.program_id(1)
    tq = ns * LANES
    nk = seq // tk
    nt = (seq // tq) * nk
    r = tk // LANES
    lam = sc_ref[h, 0]
    post = sc_ref[h, 1]

    row = lax.broadcasted_iota(jnp.int32, (LANES, LANES), 0)
    lane = lax.broadcasted_iota(jnp.int32, (LANES, LANES), 1)
    eye = (row == lane).astype(BF16)

    ones_row = (lax.broadcasted_iota(jnp.int32, (VT_ROWS - A_VDIM, tk), 0) == 0).astype(BF16)

    def vt_body(j, carry):
        vc = v_ref[0, pl.ds(pl.multiple_of(j * tk, tk), tk), :]
        vt_ref[j, :A_VDIM, :] = lax.dot_general(eye, vc, NT, preferred_element_type=F32).astype(BF16)
        vt_ref[j, A_VDIM:, :] = ones_row
        return carry

    lax.fori_loop(0, nk, vt_body, 0)

    rb = tk
    nblk = tk // rb
    rows = lambda blk: slice(blk * rb, (blk + 1) * rb)

    def logits_setup(t):
        qi = t // nk
        j = t - qi * nk
        r0 = jnp.clip((j * r - qi * ns) * LANES, -2 * tk, tk + tq)
        for s in range(ns):
            qs = q_ref[0, pl.ds(pl.multiple_of(qi * tq + s * LANES, LANES), LANES), :]
            zero = jnp.zeros_like(qs)
            qst_ref[s] = jnp.concatenate([jnp.where(lane < A_HALF, qs, zero),
                                          jnp.where(lane >= A_HALF, qs, zero)], axis=0)
        return j * tk, r0 + 2 * tk

    def logits_block(ctx, blk, dst_ref, pmax):
        krow, brow = ctx
        kc = k_ref[0, pl.ds(pl.multiple_of(krow + blk * rb, rb), rb), :]
        bias = strip_ref[0, pl.ds(pl.multiple_of(brow + blk * rb, LANES), rb), :]
        out = []
        for s in range(ns):
            b = bias[:, s * LANES:(s + 1) * LANES]
            st = (lax.dot_general(kc, qst_ref[s], NT, preferred_element_type=F32)
                  + jnp.concatenate([b, b], axis=1))
            dst_ref[s, rows(blk), :] = st
            part = jnp.max(st.reshape(4, rb // 4, 2 * LANES), axis=0)
            part = jnp.max(part.reshape(rb // 32, 8, 2 * LANES), axis=0)
            out.append(part if pmax is None else jnp.maximum(pmax[s], part))
        return tuple(out)

    def softmax_setup(t, ms, maxima):
        first = t % nk == 0
        out = []
        for s in range(ns):
            m = jnp.where(first, -1e30, ms[s])
            m_new = jnp.maximum(m, maxima[s])
            out.append((m_new, jnp.exp2(m - m_new)))
        return tuple(out)

    def softmax_block(sm, blk, src_ref, dst_ref):
        for s in range(ns):
            dst_ref[s, rows(blk), :] = jnp.exp2(src_ref[s, rows(blk), :] - sm[s][0]).astype(BF16)

    def values_block(j, blk, p_ref, pv):
        vt = vt_ref[j, :, rows(blk)]
        out = []
        for s in range(ns):
            d = jnp.dot(vt, p_ref[s, rows(blk), :], preferred_element_type=F32)
            out.append(d if pv is None else pv[s] + d)
        return tuple(out)

    def colmax(pmax):
        return tuple(jnp.max(x, axis=0, keepdims=True) for x in pmax)

    def values_finish(t, pv, alphas, may_finish):
        qi = t // nk
        j = t - qi * nk
        for s in range(ns):
            acc_ref[s] = alphas[s] * acc_ref[s] + pv[s]

        if may_finish:
            @pl.when(j == nk - 1)
            def _():
                for s in range(ns):
                    acc = acc_ref[s]
                    o = acc[:A_VDIM] * (1.0 / acc[A_VDIM:A_VDIM + 1])
                    ot = o[:, :LANES] - lam * o[:, LANES:]
                    ms = jnp.mean(ot * ot, axis=0, keepdims=True)
                    y = ot * lax.rsqrt(ms + EPS) * g_ref[...] * post
                    o_ref[0, pl.ds(pl.multiple_of(qi * tq + s * LANES, LANES), LANES), :] = y.T.astype(BF16)

    s_refs = bufs[:ATTN_BUFS]
    p_refs = bufs[ATTN_BUFS:]

    def stage(t, k, carry, may_finish):
        ms, alphas, maxima = carry
        s_w, s_r = s_refs[(k + 2) % ATTN_BUFS], s_refs[(k + 1) % ATTN_BUFS]
        p_w, p_r = p_refs[(k + 1) % ATTN_BUFS], p_refs[k]
        ctx = logits_setup(jnp.minimum(t + 2, nt - 1))
        sm = softmax_setup(jnp.minimum(t + 1, nt - 1), ms, maxima)
        j = t % nk
        pmax = pv = None
        for blk in range(nblk):
            pmax = logits_block(ctx, blk, s_w, pmax)
            softmax_block(sm, blk, s_r, p_w)
            pv = values_block(j, blk, p_r, pv)
        values_finish(t, pv, alphas, may_finish)
        return tuple(x[0] for x in sm), tuple(x[1] for x in sm), colmax(pmax)

    def body(u, carry):
        for k in range(ATTN_BUFS):
            carry = stage(ATTN_BUFS * u + k, k, carry, may_finish=(nk - 1 - k) % math.gcd(ATTN_BUFS, nk) == 0)
        return carry

    def prologue_logits(t, dst_ref):
        ctx = logits_setup(t)
        pmax = None
        for blk in range(nblk):
            pmax = logits_block(ctx, blk, dst_ref, pmax)
        return colmax(pmax)

    acc_ref[...] = jnp.zeros_like(acc_ref)
    max0 = prologue_logits(0, s_refs[0])
    max1 = prologue_logits(1, s_refs[1])
    sm0 = softmax_setup(0, (jnp.full((1, 2 * LANES), -1e30, F32),) * ns, max0)
    for blk in range(nblk):
        softmax_block(sm0, blk, s_refs[0], p_refs[0])
    init = (tuple(x[0] for x in sm0), tuple(x[1] for x in sm0), max1)
    lax.fori_loop(0, nt // ATTN_BUFS, body, init)


def _attention(p3, scal, strip, gcol, tk, ns):
    b, s, _ = p3.shape
    kern = functools.partial(_attn_kernel, seq=s, tk=tk, ns=ns)
    return pl.pallas_call(
        kern,
        grid=(b, A_HEADS),
        in_specs=[
            pl.BlockSpec(memory_space=pltpu.SMEM),
            pl.BlockSpec((1, s, LANES), lambda i, h: (i, 0, COL_QA * 8 + h)),
            pl.BlockSpec((1, s, LANES), lambda i, h: (i, 0, COL_KA * 8 + h)),
            pl.BlockSpec((1, s, LANES), lambda i, h: (i, 0, COL_VA * 8 + h)),
            pl.BlockSpec((1, 4 * tk + ns * LANES, ns * LANES), lambda i, h: (h, 0, 0)),
            pl.BlockSpec((A_VDIM, LANES), lambda i, h: (0, 0)),
        ],
        out_specs=pl.BlockSpec((1, s, LANES), lambda i, h: (i, 0, h)),
        out_shape=jax.ShapeDtypeStruct((b, s, A_WIDTH), BF16),
        scratch_shapes=([pltpu.VMEM((s // tk, VT_ROWS, tk), BF16)]
                        + [pltpu.VMEM((ns, tk, 2 * LANES), F32)] * ATTN_BUFS
                        + [pltpu.VMEM((ns, tk, 2 * LANES), BF16)] * ATTN_BUFS
                        + [pltpu.VMEM((ns, 2 * LANES, LANES), BF16)]
                        + [pltpu.VMEM((ns, VT_ROWS, 2 * LANES), F32)]),
        compiler_params=pltpu.CompilerParams(
            dimension_semantics=("parallel", "arbitrary"), vmem_limit_bytes=VMEM_LIMIT),
        name="diffattn",
    )(scal, p3, p3, p3, strip, gcol)


def _cumsum_mm(tri, x):
    hi = x.astype(BF16)
    r1 = x - hi.astype(F32)
    mid = r1.astype(BF16)
    lo = (r1 - mid.astype(F32)).astype(BF16)
    d = lambda y: jnp.dot(tri, y, preferred_element_type=F32)
    return d(hi) + d(mid) + d(lo)


def _hgrn_chunk(q, z, v, lb, tri, mask, ref_row, last_row, st_ref):
    e = jnp.exp(-jnp.abs(z))
    rcp = 1.0 / (1.0 + e)
    s_big, s_small = rcp, e * rcp
    pos = z >= 0
    sig = jnp.where(pos, s_big, s_small)
    sig_neg = jnp.where(pos, s_small, s_big)
    oml = 1.0 - lb
    logf = jnp.log2(lb + oml * sig)
    kk = oml * sig_neg
    b = _cumsum_mm(tri, logf)
    b_ref = b[ref_row:ref_row + 1, :]
    b_last = b[last_row:last_row + 1, :]
    qd = (q * jnp.exp2(b - b_ref)).astype(BF16)
    kd = (kk * jnp.exp2(b_ref - b)).astype(BF16)
    kl = (kk * jnp.exp2(b_last - b)).astype(BF16)
    qe = (q * jnp.exp2(b)).astype(BF16)
    dec = jnp.exp2(b_last)
    vb = v
    outs = []
    for h in range(B_HEADS):
        sl = slice(h * B_DIM, (h + 1) * B_DIM)
        a = lax.dot_general(qd[:, sl], kd[:, sl], NT, preferred_element_type=F32)
        a = jnp.where(mask, a, 0.0).astype(BF16)
        intra = jnp.dot(a, vb[:, sl], preferred_element_type=F32)
        st = st_ref[h]
        inter = lax.dot_general(qe[:, sl], st.astype(BF16), NT, preferred_element_type=F32)
        outs.append(intra + inter)
        upd = lax.dot_general(vb[:, sl], kl[:, sl], TN, preferred_element_type=F32)
        st_ref[h] = st * dec[:, sl] + upd
    return jnp.concatenate(outs, axis=1)


def _hgrn_kernel(lb_ref, qf_ref, zf_ref, vf_ref, qb_ref, zb_ref, vb_ref, of_ref, ob_ref,
                 stf_ref, stb_ref, *, tc):
    @pl.when(pl.program_id(1) == 0)
    def _():
        stf_ref[...] = jnp.zeros_like(stf_ref)
        stb_ref[...] = jnp.zeros_like(stb_ref)

    nch = tc // CHUNK
    row = lax.broadcasted_iota(jnp.int32, (CHUNK, CHUNK), 0)
    col = lax.broadcasted_iota(jnp.int32, (CHUNK, CHUNK), 1)
    mask_f = col <= row
    mask_b = col >= row
    tri_f = mask_f.astype(BF16)
    tri_b = mask_b.astype(BF16)
    lb_f = lb_ref[0:1, :]
    lb_b = lb_ref[1:2, :]

    def body(c, carry):
        rf = pl.ds(pl.multiple_of(c * CHUNK, CHUNK), CHUNK)
        rb = pl.ds(pl.multiple_of((nch - 1 - c) * CHUNK, CHUNK), CHUNK)
        ld = lambda ref, rows: ref[0, rows, :].astype(F32)
        of_ref[0, rf, :] = _hgrn_chunk(ld(qf_ref, rf), ld(zf_ref, rf), vf_ref[0, rf, :], lb_f, tri_f, mask_f,
                                       CHUNK // 2 - 1, CHUNK - 1, stf_ref).astype(of_ref.dtype)
        ob_ref[0, rb, :] = _hgrn_chunk(ld(qb_ref, rb), ld(zb_ref, rb), vb_ref[0, rb, :], lb_b, tri_b, mask_b,
                                       CHUNK // 2, 0, stb_ref).astype(ob_ref.dtype)
        return carry

    lax.fori_loop(0, nch, body, 0, unroll=2)


def _hgrn(p3, lb2, tc):
    b, s, _ = p3.shape
    nb = s // tc
    fwd = lambda col: pl.BlockSpec((1, tc, B_WIDTH), lambda i, t: (i, t, col))
    bwd = lambda col: pl.BlockSpec((1, tc, B_WIDTH), lambda i, t: (i, nb - 1 - t, col))
    return pl.pallas_call(
        functools.partial(_hgrn_kernel, tc=tc),
        grid=(b, nb),
        in_specs=[
            pl.BlockSpec((2, B_WIDTH), lambda i, t: (0, 0)),
            fwd(COL_QB), fwd(COL_ZF), fwd(COL_VB),
            bwd(COL_QB), bwd(COL_ZB), bwd(COL_VB),
        ],
        out_specs=[
            pl.BlockSpec((1, tc, B_WIDTH), lambda i, t: (i, t, 0)),
            pl.BlockSpec((1, tc, B_WIDTH), lambda i, t: (i, nb - 1 - t, 0)),
        ],
        out_shape=[jax.ShapeDtypeStruct((b, s, B_WIDTH), BF16)] * 2,
        scratch_shapes=[pltpu.VMEM((B_HEADS, B_DIM, B_DIM), F32)] * 2,
        compiler_params=pltpu.CompilerParams(
            dimension_semantics=("parallel", "arbitrary"), vmem_limit_bytes=VMEM_LIMIT),
        name="hgrn2",
    )(lb2, p3, p3, p3, p3, p3, p3)


def _merge_kernel(x_ref, ya_ref, of_ref, ob_ref, gb_ref, gate_ref, kv_ref,
                  wa_ref, wb_ref, wo_ref, wq_ref, wox_ref,
                  ghg_ref, gpm_ref, gpx_ref, gpox_ref, o_ref):
    o = of_ref[...].astype(F32) + ob_ref[...].astype(F32)
    gb = gb_ref[...].astype(F32)
    ys = []
    for h in range(B_HEADS):
        sl = slice(h * B_DIM, (h + 1) * B_DIM)
        gh = gb[:, sl]
        ys.append((_rms(o[:, sl], ghg_ref[...]) * (gh * _sigmoid(gh))).astype(BF16))
    yb = jnp.concatenate(ys, axis=1)
    br_a = jnp.dot(ya_ref[...], wa_ref[...], preferred_element_type=F32)
    br_b = jnp.dot(yb, wb_ref[...], preferred_element_type=F32)
    g = gate_ref[...].astype(F32)
    mix = g[:, :D_MODEL] * br_a + g[:, D_MODEL:] * br_b
    t = jnp.dot(mix.astype(BF16), wo_ref[...], preferred_element_type=F32)
    x1 = x_ref[...] + _rms(t, gpm_ref[...])

    h2 = _rms(x1, gpx_ref[...]).astype(BF16)
    qx = (jnp.dot(h2, wq_ref[...], preferred_element_type=F32) * (X_DIM ** -0.5 * LOG2E)).astype(BF16)
    kv = kv_ref[0]
    oxs = []
    for h in range(X_HEADS):
        sl = slice(h * X_DIM, (h + 1) * X_DIM)
        lg = lax.dot_general(qx[:, sl], kv[:, sl], NT, preferred_element_type=F32)
        p = jnp.exp2(lg - jnp.max(lg, axis=-1, keepdims=True))
        l = jnp.sum(p, axis=-1, keepdims=True)
        vh = kv[:, X_WIDTH + h * X_DIM:X_WIDTH + (h + 1) * X_DIM]
        oxs.append((jnp.dot(p.astype(BF16), vh, preferred_element_type=F32) * (1.0 / l)).astype(BF16))
    ox = jnp.concatenate(oxs, axis=1)
    t2 = jnp.dot(ox, wox_ref[...], preferred_element_type=F32)
    o_ref[...] = x1 + _rms(t2, gpox_ref[...])


def _merge(x2d, ya2d, of2d, ob2d, p2d, kvx, wa, wb, wo, wq, wox, ghg, gpm, gpx, gpox, seq, tm):
    t = x2d.shape[0]
    per_seq = seq // tm
    const = lambda shape: pl.BlockSpec(shape, lambda i: (0,) * len(shape), pipeline_mode=pl.Buffered(1))
    return pl.pallas_call(
        _merge_kernel,
        grid=(t // tm,),
        in_specs=[
            pl.BlockSpec((tm, D_MODEL), lambda i: (i, 0)),
            pl.BlockSpec((tm, A_WIDTH), lambda i: (i, 0)),
            pl.BlockSpec((tm, B_WIDTH), lambda i: (i, 0)),
            pl.BlockSpec((tm, B_WIDTH), lambda i: (i, 0)),
            pl.BlockSpec((tm, B_WIDTH), lambda i: (i, COL_GB)),
            pl.BlockSpec((tm, 2 * D_MODEL), lambda i: (i, COL_GATE // 4)),
            pl.BlockSpec((1,) + kvx.shape[1:], lambda i: (i // per_seq, 0, 0)),
            const((A_WIDTH, D_MODEL)), const((B_WIDTH, D_MODEL)), const((D_MODEL, D_MODEL)),
            const((D_MODEL, X_WIDTH)), const((X_WIDTH, D_MODEL)),
            const((1, B_DIM)), const((1, D_MODEL)), const((1, D_MODEL)), const((1, D_MODEL)),
        ],
        out_specs=pl.BlockSpec((tm, D_MODEL), lambda i: (i, 0)),
        out_shape=jax.ShapeDtypeStruct((t, D_MODEL), F32),
        compiler_params=pltpu.CompilerParams(
            dimension_semantics=("parallel",), vmem_limit_bytes=VMEM_LIMIT),
        name="merge_xattn",
    )(x2d, ya2d, of2d, ob2d, p2d, p2d, kvx, wa, wb, wo, wq, wox, ghg, gpm, gpx, gpox)


def _ffn_kernel(x_ref, g_ref, wg_ref, wu_ref, wd_ref, gpost_ref, o_ref, h_ref, *, nf):
    f = pl.program_id(1)

    @pl.when(f == 0)
    def _():
        h_ref[...] = _rms(x_ref[...], g_ref[...]).astype(BF16)

    h = h_ref[...]
    gt = jnp.dot(h, wg_ref[...], preferred_element_type=F32)
    up = jnp.dot(h, wu_ref[...], preferred_element_type=F32)
    act = (gt * _sigmoid(gt) * up).astype(BF16)
    part = jnp.dot(act, wd_ref[...], preferred_element_type=F32)

    @pl.when(f == 0)
    def _():
        o_ref[...] = part

    @pl.when(jnp.logical_and(f > 0, f < nf - 1))
    def _():
        o_ref[...] += part

    @pl.when(f == nf - 1)
    def _():
        o_ref[...] = x_ref[...] + _rms(o_ref[...] + part, gpost_ref[...])


def _ffn(x2d, g, wgu, wd, gpost, tm, tf):
    t = x2d.shape[0]
    nf = D_FF // tf
    return pl.pallas_call(
        functools.partial(_ffn_kernel, nf=nf),
        grid=(t // tm, nf),
        in_specs=[
            pl.BlockSpec((tm, D_MODEL), lambda i, f: (i, 0)),
            pl.BlockSpec((1, D_MODEL), lambda i, f: (0, 0)),
            pl.BlockSpec((D_MODEL, tf), lambda i, f: (0, f)),
            pl.BlockSpec((D_MODEL, tf), lambda i, f: (0, nf + f)),
            pl.BlockSpec((tf, D_MODEL), lambda i, f: (f, 0)),
            pl.BlockSpec((1, D_MODEL), lambda i, f: (0, 0)),
        ],
        out_specs=pl.BlockSpec((tm, D_MODEL), lambda i, f: (i, 0)),
        out_shape=jax.ShapeDtypeStruct((t, D_MODEL), F32),
        scratch_shapes=[pltpu.VMEM((tm, D_MODEL), BF16)],
        compiler_params=pltpu.CompilerParams(
            dimension_semantics=("parallel", "arbitrary"), vmem_limit_bytes=VMEM_LIMIT),
        name="swiglu",
    )(x2d, g, wgu, wgu, wd, gpost)


def _rel_bucket(rel):
    nb = REL_BUCKETS // 2
    max_exact = nb // 2
    n = jnp.abs(rel)
    nf = jnp.maximum(n, 1).astype(jnp.float32)
    large = max_exact + (jnp.log(nf / max_exact) / math.log(REL_MAX_DIST / max_exact) * (nb - max_exact)).astype(jnp.int32)
    large = jnp.minimum(large, nb - 1)
    return jnp.where(rel > 0, nb, 0) + jnp.where(n < max_exact, n, large)


def _tiles(seq):
    tk = min(1024, seq)
    ns = 2
    tc = min(256, seq)
    tm_proj = min(1024, seq)
    tm_merge = min(256, seq)
    tm_ffn = min(512, seq)
    return tk, ns, tc, tm_proj, tm_merge, tm_ffn


def _layer(x, mem, w, seq_tiles):
    b, s, _ = x.shape
    tk, ns, tc, tm_proj, tm_merge, tm_ffn = seq_tiles
    x2d = x.reshape(b * s, D_MODEL)
    p2d = _inproj(x2d, w["g_pre_mix"], w["w_in"], w["colbias"], tm_proj)
    p3 = p2d.reshape(b, s, N_IN)
    ya = _attention(p3, w["attn_scal"], w["strip"][(tk, ns)], w["g_subln_col"], tk, ns)
    o_f, o_b = _hgrn(p3, w["lb2"], tc)
    kvx = _memkv(mem, w["g_mem"], w["w_kv_x"])
    x2 = _merge(x2d, ya.reshape(b * s, A_WIDTH), o_f.reshape(b * s, B_WIDTH), o_b.reshape(b * s, B_WIDTH),
                p2d, kvx, w["w_branch_a"], w["w_branch_b"], w["w_out"], w["w_q_x"], w["w_o_x"],
                w["g_hgrn_norm"], w["g_post_mix"], w["g_pre_x"], w["g_post_x"], s, tm_merge)
    y = _ffn(x2, w["g_pre_ffn"], w["w_gate_up"], w["w_down"], w["g_post_ffn"], tm_ffn, 512)
    return y.reshape(b, s, D_MODEL)


def _bias_strip(rel_bias, tk, tq):
    r = jnp.arange(-2 * tk, 2 * tk + tq, dtype=jnp.int32)[:, None]
    i = jnp.arange(tq, dtype=jnp.int32)[None, :]
    onehot = (_rel_bucket(r - i)[:, :, None] == jnp.arange(REL_BUCKETS, dtype=jnp.int32)).astype(F32)
    return jnp.einsum("rik,kh->hri", onehot, rel_bias.astype(F32) * LOG2E, precision=lax.Precision.HIGHEST)


def kernel(x_prompt, x_sample, mem_prompt, mem_sample, rel_bias, hgrn_lb_logits, g_pre_mix, w_in, b_merge,
           lam_q1, lam_k1, lam_q2, lam_k2, g_subln, g_hgrn_norm, w_branch_a, w_branch_b, w_out, g_post_mix,
           g_pre_x, g_mem, w_q_x, w_kv_x, w_o_x, g_post_x, g_pre_ffn, w_gate_up, w_down, g_post_ffn):
    l = 0
    row = lambda a: a[l].astype(F32).reshape(1, -1)
    lam = (jnp.exp(jnp.sum(lam_q1[l].astype(F32) * lam_k1[l].astype(F32)))
           - jnp.exp(jnp.sum(lam_q2[l].astype(F32) * lam_k2[l].astype(F32))) + LAM_INIT)
    lb_all = jnp.cumsum(jax.nn.softmax(hgrn_lb_logits.astype(F32), axis=1), axis=1)
    ones = jnp.ones((A_HEADS,), F32)
    attn_scal = jnp.stack([lam * ones, (1.0 - LAM_INIT) * ones], axis=1)
    tiles_p = _tiles(x_prompt.shape[1])
    tiles_s = _tiles(x_sample.shape[1])
    w = dict(
        g_pre_mix=row(g_pre_mix), w_in=w_in[l].astype(BF16),
        colbias=jnp.concatenate([jnp.zeros((1, N_IN - 2 * D_MODEL), F32), row(b_merge)], axis=1),
        attn_scal=attn_scal,
        strip={(tk, ns): _bias_strip(rel_bias, tk, ns * LANES) for tk, ns in {tiles_p[:2], tiles_s[:2]}},
        g_subln_col=jnp.broadcast_to(g_subln[l].astype(F32)[:, None], (A_VDIM, LANES)),
        lb2=jnp.stack([lb_all[0, l], lb_all[1, l]], axis=0),
        g_mem=row(g_mem), w_kv_x=w_kv_x[l].astype(BF16),
        w_branch_a=w_branch_a[l].astype(BF16), w_branch_b=w_branch_b[l].astype(BF16),
        w_out=w_out[l].astype(BF16), w_q_x=w_q_x[l].astype(BF16), w_o_x=w_o_x[l].astype(BF16),
        g_hgrn_norm=row(g_hgrn_norm), g_post_mix=row(g_post_mix), g_pre_x=row(g_pre_x), g_post_x=row(g_post_x),
        g_pre_ffn=row(g_pre_ffn), w_gate_up=w_gate_up[l].astype(BF16), w_down=w_down[l].astype(BF16),
        g_post_ffn=row(g_post_ffn),
    )
    return (_layer(x_prompt, mem_prompt, w, tiles_p), _layer(x_sample, mem_sample, w, tiles_s))
```

```python
import functools
import math

import jax
import jax.numpy as jnp
import numpy as np
from jax import lax
from jax.experimental import pallas as pl
from jax.experimental.pallas import tpu as pltpu

F32 = jnp.float32
BF16 = jnp.bfloat16

D_MODEL = 2048
A_WIDTH = 1024
A_HALF = 64
A_HEADS = 8
A_VDIM = 128
B_WIDTH = 1024
B_DIM = 128
B_HEADS = 8
CHUNK = 64
REL_BUCKETS = 32
REL_MAX_DIST = 128
X_HEADS = 4
X_DIM = 128
X_WIDTH = 512
D_FF = 5632
N_IN = 12288
EPS = 1e-6
LOG2E = 1.4426950408889634
LAM_INIT = 0.8 - 0.6 * math.exp(-0.3 * 0)

LANES = 128
VT_ROWS = A_VDIM + 16
ATTN_BUFS = 2
VMEM_LIMIT = 56 * 1024 * 1024

COL_QA, COL_KA, COL_VA, COL_QB, COL_ZF, COL_ZB, COL_VB, COL_GB, COL_GATE = range(9)

NT = (((1,), (1,)), ((), ()))
TN = (((0,), (0,)), ((), ()))


def _rms(x, g):
    ms = jnp.mean(x * x, axis=-1, keepdims=True)
    return x * lax.rsqrt(ms + EPS) * g


def _sigmoid(x):
    return 1.0 / (1.0 + jnp.exp(-x))


def _colmax(x):
    while x.shape[0] > 64:
        x = jnp.max(x.reshape(8, x.shape[0] // 8, x.shape[1]), axis=0)
    return jnp.max(x, axis=0, keepdims=True)


def _inproj_kernel(x_ref, g_ref, w_ref, cb_ref, o_ref, h_ref):
    j = pl.program_id(1)

    @pl.when(j == 0)
    def _():
        h_ref[...] = _rms(x_ref[...], g_ref[...]).astype(BF16)

    @pl.when(j < COL_GATE)
    def _():
        scale = jnp.where(j == COL_QA, A_HALF ** -0.5 * LOG2E, jnp.where(j == COL_QB, B_DIM ** -0.5, 1.0))
        acc = jnp.dot(h_ref[...], w_ref[...], preferred_element_type=F32)
        o_ref[...] = (acc * scale).astype(BF16)

    @pl.when(j >= COL_GATE)
    def _():
        acc = jnp.dot(h_ref[...], w_ref[...], preferred_element_type=F32)
        o_ref[...] = _sigmoid(acc + cb_ref[...]).astype(BF16)


def _inproj(x2d, g, w, colbias, tm):
    t = x2d.shape[0]
    tn = 1024
    return pl.pallas_call(
        _inproj_kernel,
        grid=(t // tm, N_IN // tn),
        in_specs=[
            pl.BlockSpec((tm, D_MODEL), lambda i, j: (i, 0)),
            pl.BlockSpec((1, D_MODEL), lambda i, j: (0, 0)),
            pl.BlockSpec((D_MODEL, tn), lambda i, j: (0, j)),
            pl.BlockSpec((1, tn), lambda i, j: (0, j)),
        ],
        out_specs=pl.BlockSpec((tm, tn), lambda i, j: (i, j)),
        out_shape=jax.ShapeDtypeStruct((t, N_IN), BF16),
        scratch_shapes=[pltpu.VMEM((tm, D_MODEL), BF16)],
        compiler_params=pltpu.CompilerParams(
            dimension_semantics=("parallel", "arbitrary"), vmem_limit_bytes=VMEM_LIMIT),
        name="inproj",
    )(x2d, g, w, colbias)


def _memkv_kernel(m_ref, g_ref, w_ref, o_ref):
    h = _rms(m_ref[0], g_ref[...]).astype(BF16)
    o_ref[0] = jnp.dot(h, w_ref[...], preferred_element_type=F32).astype(BF16)


def _memkv(mem, g, w):
    b, n, _ = mem.shape
    return pl.pallas_call(
        _memkv_kernel,
        grid=(b,),
        in_specs=[
            pl.BlockSpec((1, n, D_MODEL), lambda i: (i, 0, 0)),
            pl.BlockSpec((1, D_MODEL), lambda i: (0, 0)),
            pl.BlockSpec((D_MODEL, 2 * X_WIDTH), lambda i: (0, 0)),
        ],
        out_specs=pl.BlockSpec((1, n, 2 * X_WIDTH), lambda i: (i, 0, 0)),
        out_shape=jax.ShapeDtypeStruct((b, n, 2 * X_WIDTH), BF16),
        compiler_params=pltpu.CompilerParams(
            dimension_semantics=("parallel",), vmem_limit_bytes=VMEM_LIMIT),
        name="memkv",
    )(mem, g, w)


def _attn_kernel(sc_ref, q_ref, k_ref, v_ref, strip_ref, g_ref, o_ref, vt_ref, *scratch, seq, tk, ns):
    bufs, qst_ref, acc_ref = scratch[:-2], scratch[-2], scratch[-1]
    h = pl.program_id(1)
    tq = ns * LANES
    nk = seq // tk
    nt = (seq // tq) * nk
    r = tk // LANES
    lam = sc_ref[h, 0]
    post = sc_ref[h, 1]

    row = lax.broadcasted_iota(jnp.int32, (LANES, LANES), 0)
    lane = lax.broadcasted_iota(jnp.int32, (LANES, LANES), 1)
    eye = (row == lane).astype(BF16)

    ones_row = (lax.broadcasted_iota(jnp.int32, (VT_ROWS - A_VDIM, tk), 0) == 0).astype(BF16)

    def vt_body(j, carry):
        vc = v_ref[0, pl.ds(pl.multiple_of(j * tk, tk), tk), :]
        vt_ref[j, :A_VDIM, :] = lax.dot_general(eye, vc, NT, preferred_element_type=F32).astype(BF16)
        vt_ref[j, A_VDIM:, :] = ones_row
        return carry

    lax.fori_loop(0, nk, vt_body, 0)

    rb = tk
    nblk = tk // rb
    rows = lambda blk: slice(blk * rb, (blk + 1) * rb)

    def logits_setup(t):
        qi = t // nk
        j = t - qi * nk
        r0 = jnp.clip((j * r - qi * ns) * LANES, -2 * tk, tk + tq)
        for s in range(ns):
            qs = q_ref[0, pl.ds(pl.multiple_of(qi * tq + s * LANES, LANES), LANES), :]
            zero = jnp.zeros_like(qs)
            qst_ref[s] = jnp.concatenate([jnp.where(lane < A_HALF, qs, zero),
                                          jnp.where(lane >= A_HALF, qs, zero)], axis=0)
        return j * tk, r0 + 2 * tk

    def logits_block(ctx, blk, dst_ref, pmax):
        krow, brow = ctx
        kc = k_ref[0, pl.ds(pl.multiple_of(krow + blk * rb, rb), rb), :]
        bias = strip_ref[0, pl.ds(pl.multiple_of(brow + blk * rb, LANES), rb), :]
        out = []
        for s in range(ns):
            b = bias[:, s * LANES:(s + 1) * LANES]
            st = (lax.dot_general(kc, qst_ref[s], NT, preferred_element_type=F32)
                  + jnp.concatenate([b, b], axis=1))
            dst_ref[s, rows(blk), :] = st
            part = jnp.max(st.reshape(4, rb // 4, 2 * LANES), axis=0)
            part = jnp.max(part.reshape(rb // 32, 8, 2 * LANES), axis=0)
            out.append(part if pmax is None else jnp.maximum(pmax[s], part))
        return tuple(out)

    def softmax_setup(t, ms, maxima):
        first = t % nk == 0
        out = []
        for s in range(ns):
            m = jnp.where(first, -1e30, ms[s])
            m_new = jnp.maximum(m, maxima[s])
            out.append((m_new, jnp.exp2(m - m_new)))
        return tuple(out)

    def softmax_block(sm, blk, src_ref, dst_ref):
        for s in range(ns):
            dst_ref[s, rows(blk), :] = jnp.exp2(src_ref[s, rows(blk), :] - sm[s][0]).astype(BF16)

    def values_block(j, blk, p_ref, pv):
        vt = vt_ref[j, :, rows(blk)]
        out = []
        for s in range(ns):
            d = jnp.dot(vt, p_ref[s, rows(blk), :], preferred_element_type=F32)
            out.append(d if pv is None else pv[s] + d)
        return tuple(out)

    def colmax(pmax):
        return tuple(jnp.max(x, axis=0, keepdims=True) for x in pmax)

    def values_finish(t, pv, alphas, may_finish):
        qi = t // nk
        j = t - qi * nk
        for s in range(ns):
            acc_ref[s] = alphas[s] * acc_ref[s] + pv[s]

        if may_finish:
            @pl.when(j == nk - 1)
            def _():
                for s in range(ns):
                    acc = acc_ref[s]
                    o = acc[:A_VDIM] * (1.0 / acc[A_VDIM:A_VDIM + 1])
                    ot = o[:, :LANES] - lam * o[:, LANES:]
                    ms = jnp.mean(ot * ot, axis=0, keepdims=True)
                    y = ot * lax.rsqrt(ms + EPS) * g_ref[...] * post
                    o_ref[0, pl.ds(pl.multiple_of(qi * tq + s * LANES, LANES), LANES), :] = y.T.astype(BF16)

    s_refs = bufs[:ATTN_BUFS]
    p_refs = bufs[ATTN_BUFS:]

    def stage(t, k, carry, may_finish):
        ms, alphas, maxima = carry
        s_w, s_r = s_refs[(k + 2) % ATTN_BUFS], s_refs[(k + 1) % ATTN_BUFS]
        p_w, p_r = p_refs[(k + 1) % ATTN_BUFS], p_refs[k]
        ctx = logits_setup(jnp.minimum(t + 2, nt - 1))
        sm = softmax_setup(jnp.minimum(t + 1, nt - 1), ms, maxima)
        j = t % nk
        pmax = pv = None
        for blk in range(nblk):
            pmax = logits_block(ctx, blk, s_w, pmax)
            softmax_block(sm, blk, s_r, p_w)
            pv = values_block(j, blk, p_r, pv)
        values_finish(t, pv, alphas, may_finish)
        return tuple(x[0] for x in sm), tuple(x[1] for x in sm), colmax(pmax)

    def body(u, carry):
        for k in range(ATTN_BUFS):
            carry = stage(ATTN_BUFS * u + k, k, carry, may_finish=(nk - 1 - k) % math.gcd(ATTN_BUFS, nk) == 0)
        return carry

    def prologue_logits(t, dst_ref):
        ctx = logits_setup(t)
        pmax = None
        for blk in range(nblk):
            pmax = logits_block(ctx, blk, dst_ref, pmax)
        return colmax(pmax)

    acc_ref[...] = jnp.zeros_like(acc_ref)
    max0 = prologue_logits(0, s_refs[0])
    max1 = prologue_logits(1, s_refs[1])
    sm0 = softmax_setup(0, (jnp.full((1, 2 * LANES), -1e30, F32),) * ns, max0)
    for blk in range(nblk):
        softmax_block(sm0, blk, s_refs[0], p_refs[0])
    init = (tuple(x[0] for x in sm0), tuple(x[1] for x in sm0), max1)
    lax.fori_loop(0, nt // ATTN_BUFS, body, init)


def _attention(p3, scal, strip, gcol, tk, ns):
    b, s, _ = p3.shape
    kern = functools.partial(_attn_kernel, seq=s, tk=tk, ns=ns)
    return pl.pallas_call(
        kern,
        grid=(b, A_HEADS),
        in_specs=[
            pl.BlockSpec(memory_space=pltpu.SMEM),
            pl.BlockSpec((1, s, LANES), lambda i, h: (i, 0, COL_QA * 8 + h)),
            pl.BlockSpec((1, s, LANES), lambda i, h: (i, 0, COL_KA * 8 + h)),
            pl.BlockSpec((1, s, LANES), lambda i, h: (i, 0, COL_VA * 8 + h)),
            pl.BlockSpec((1, 4 * tk + ns * LANES, ns * LANES), lambda i, h: (h, 0, 0)),
            pl.BlockSpec((A_VDIM, LANES), lambda i, h: (0, 0)),
        ],
        out_specs=pl.BlockSpec((1, s, LANES), lambda i, h: (i, 0, h)),
        out_shape=jax.ShapeDtypeStruct((b, s, A_WIDTH), BF16),
        scratch_shapes=([pltpu.VMEM((s // tk, VT_ROWS, tk), BF16)]
                        + [pltpu.VMEM((ns, tk, 2 * LANES), F32)] * ATTN_BUFS
                        + [pltpu.VMEM((ns, tk, 2 * LANES), BF16)] * ATTN_BUFS
                        + [pltpu.VMEM((ns, 2 * LANES, LANES), BF16)]
                        + [pltpu.VMEM((ns, VT_ROWS, 2 * LANES), F32)]),
        compiler_params=pltpu.CompilerParams(
            dimension_semantics=("parallel", "arbitrary"), vmem_limit_bytes=VMEM_LIMIT),
        name="diffattn",
    )(scal, p3, p3, p3, strip, gcol)


def _cumsum_mm(tri, x):
    hi = x.astype(BF16)
    r1 = x - hi.astype(F32)
    mid = r1.astype(BF16)
    lo = (r1 - mid.astype(F32)).astype(BF16)
    d = lambda y: jnp.dot(tri, y, preferred_element_type=F32)
    return d(hi) + d(mid) + d(lo)


def _hgrn_chunk(q, z, v, lb, tri, mask, ref_row, last_row, st_ref):
    e = jnp.exp(-jnp.abs(z))
    rcp = 1.0 / (1.0 + e)
    s_big, s_small = rcp, e * rcp
    pos = z >= 0
    sig = jnp.where(pos, s_big, s_small)
    sig_neg = jnp.where(pos, s_small, s_big)
    oml = 1.0 - lb
    logf = jnp.log2(lb + oml * sig)
    kk = oml * sig_neg
    b = _cumsum_mm(tri, logf)
    b_ref = b[ref_row:ref_row + 1, :]
    b_last = b[last_row:last_row + 1, :]
    qd = (q * jnp.exp2(b - b_ref)).astype(BF16)
    kd = (kk * jnp.exp2(b_ref - b)).astype(BF16)
    kl = (kk * jnp.exp2(b_last - b)).astype(BF16)
    qe = (q * jnp.exp2(b)).astype(BF16)
    dec = jnp.exp2(b_last)
    vb = v
    outs = []
    for h in range(B_HEADS):
        sl = slice(h * B_DIM, (h + 1) * B_DIM)
        a = lax.dot_general(qd[:, sl], kd[:, sl], NT, preferred_element_type=F32)
        a = jnp.where(mask, a, 0.0).astype(BF16)
        intra = jnp.dot(a, vb[:, sl], preferred_element_type=F32)
        st = st_ref[h]
        inter = lax.dot_general(qe[:, sl], st.astype(BF16), NT, preferred_element_type=F32)
        outs.append(intra + inter)
        upd = lax.dot_general(vb[:, sl], kl[:, sl], TN, preferred_element_type=F32)
        st_ref[h] = st * dec[:, sl] + upd
    return jnp.concatenate(outs, axis=1)


def _hgrn_kernel(lb_ref, qf_ref, zf_ref, vf_ref, qb_ref, zb_ref, vb_ref, of_ref, ob_ref,
                 stf_ref, stb_ref, *, tc):
    @pl.when(pl.program_id(1) == 0)
    def _():
        stf_ref[...] = jnp.zeros_like(stf_ref)
        stb_ref[...] = jnp.zeros_like(stb_ref)

    nch = tc // CHUNK
    row = lax.broadcasted_iota(jnp.int32, (CHUNK, CHUNK), 0)
    col = lax.broadcasted_iota(jnp.int32, (CHUNK, CHUNK), 1)
    mask_f = col <= row
    mask_b = col >= row
    tri_f = mask_f.astype(BF16)
    tri_b = mask_b.astype(BF16)
    lb_f = lb_ref[0:1, :]
    lb_b = lb_ref[1:2, :]

    def body(c, carry):
        rf = pl.ds(pl.multiple_of(c * CHUNK, CHUNK), CHUNK)
        rb = pl.ds(pl.multiple_of((nch - 1 - c) * CHUNK, CHUNK), CHUNK)
        ld = lambda ref, rows: ref[0, rows, :].astype(F32)
        of_ref[0, rf, :] = _hgrn_chunk(ld(qf_ref, rf), ld(zf_ref, rf), vf_ref[0, rf, :], lb_f, tri_f, mask_f,
                                       CHUNK // 2 - 1, CHUNK - 1, stf_ref).astype(of_ref.dtype)
        ob_ref[0, rb, :] = _hgrn_chunk(ld(qb_ref, rb), ld(zb_ref, rb), vb_ref[0, rb, :], lb_b, tri_b, mask_b,
                                       CHUNK // 2, 0, stb_ref).astype(ob_ref.dtype)
        return carry

    lax.fori_loop(0, nch, body, 0, unroll=2)


def _hgrn(p3, lb2, tc):
    b, s, _ = p3.shape
    nb = s // tc
    fwd = lambda col: pl.BlockSpec((1, tc, B_WIDTH), lambda i, t: (i, t, col))
    bwd = lambda col: pl.BlockSpec((1, tc, B_WIDTH), lambda i, t: (i, nb - 1 - t, col))
    return pl.pallas_call(
        functools.partial(_hgrn_kernel, tc=tc),
        grid=(b, nb),
        in_specs=[
            pl.BlockSpec((2, B_WIDTH), lambda i, t: (0, 0)),
            fwd(COL_QB), fwd(COL_ZF), fwd(COL_VB),
            bwd(COL_QB), bwd(COL_ZB), bwd(COL_VB),
        ],
        out_specs=[
            pl.BlockSpec((1, tc, B_WIDTH), lambda i, t: (i, t, 0)),
            pl.BlockSpec((1, tc, B_WIDTH), lambda i, t: (i, nb - 1 - t, 0)),
        ],
        out_shape=[jax.ShapeDtypeStruct((b, s, B_WIDTH), BF16)] * 2,
        scratch_shapes=[pltpu.VMEM((B_HEADS, B_DIM, B_DIM), F32)] * 2,
        compiler_params=pltpu.CompilerParams(
            dimension_semantics=("parallel", "arbitrary"), vmem_limit_bytes=VMEM_LIMIT),
        name="hgrn2",
    )(lb2, p3, p3, p3, p3, p3, p3)


def _merge_kernel(x_ref, ya_ref, of_ref, ob_ref, gb_ref, gate_ref, kv_ref,
                  wa_ref, wb_ref, wo_ref, wq_ref, wox_ref,
                  ghg_ref, gpm_ref, gpx_ref, gpox_ref, o_ref):
    o = of_ref[...].astype(F32) + ob_ref[...].astype(F32)
    gb = gb_ref[...].astype(F32)
    ys = []
    for h in range(B_HEADS):
        sl = slice(h * B_DIM, (h + 1) * B_DIM)
        gh = gb[:, sl]
        ys.append((_rms(o[:, sl], ghg_ref[...]) * (gh * _sigmoid(gh))).astype(BF16))
    yb = jnp.concatenate(ys, axis=1)
    br_a = jnp.dot(ya_ref[...], wa_ref[...], preferred_element_type=F32)
    br_b = jnp.dot(yb, wb_ref[...], preferred_element_type=F32)
    g = gate_ref[...].astype(F32)
    mix = g[:, :D_MODEL] * br_a + g[:, D_MODEL:] * br_b
    t = jnp.dot(mix.astype(BF16), wo_ref[...], preferred_element_type=F32)
    x1 = x_ref[...] + _rms(t, gpm_ref[...])

    h2 = _rms(x1, gpx_ref[...]).astype(BF16)
    qx = (jnp.dot(h2, wq_ref[...], preferred_element_type=F32) * (X_DIM ** -0.5 * LOG2E)).astype(BF16)
    kv = kv_ref[0]
    oxs = []
    for h in range(X_HEADS):
        sl = slice(h * X_DIM, (h + 1) * X_DIM)
        lg = lax.dot_general(qx[:, sl], kv[:, sl], NT, preferred_element_type=F32)
        p = jnp.exp2(lg - jnp.max(lg, axis=-1, keepdims=True))
        l = jnp.sum(p, axis=-1, keepdims=True)
        vh = kv[:, X_WIDTH + h * X_DIM:X_WIDTH + (h + 1) * X_DIM]
        oxs.append((jnp.dot(p.astype(BF16), vh, preferred_element_type=F32) * (1.0 / l)).astype(BF16))
    ox = jnp.concatenate(oxs, axis=1)
    t2 = jnp.dot(ox, wox_ref[...], preferred_element_type=F32)
    o_ref[...] = x1 + _rms(t2, gpox_ref[...])


def _merge(x2d, ya2d, of2d, ob2d, p2d, kvx, wa, wb, wo, wq, wox, ghg, gpm, gpx, gpox, seq, tm):
    t = x2d.shape[0]
    per_seq = seq // tm
    const = lambda shape: pl.BlockSpec(shape, lambda i: (0,) * len(shape), pipeline_mode=pl.Buffered(1))
    return pl.pallas_call(
        _merge_kernel,
        grid=(t // tm,),
        in_specs=[
            pl.BlockSpec((tm, D_MODEL), lambda i: (i, 0)),
            pl.BlockSpec((tm, A_WIDTH), lambda i: (i, 0)),
            pl.BlockSpec((tm, B_WIDTH), lambda i: (i, 0)),
            pl.BlockSpec((tm, B_WIDTH), lambda i: (i, 0)),
            pl.BlockSpec((tm, B_WIDTH), lambda i: (i, COL_GB)),
            pl.BlockSpec((tm, 2 * D_MODEL), lambda i: (i, COL_GATE // 4)),
            pl.BlockSpec((1,) + kvx.shape[1:], lambda i: (i // per_seq, 0, 0)),
            const((A_WIDTH, D_MODEL)), const((B_WIDTH, D_MODEL)), const((D_MODEL, D_MODEL)),
            const((D_MODEL, X_WIDTH)), const((X_WIDTH, D_MODEL)),
            const((1, B_DIM)), const((1, D_MODEL)), const((1, D_MODEL)), const((1, D_MODEL)),
        ],
        out_specs=pl.BlockSpec((tm, D_MODEL), lambda i: (i, 0)),
        out_shape=jax.ShapeDtypeStruct((t, D_MODEL), F32),
        compiler_params=pltpu.CompilerParams(
            dimension_semantics=("parallel",), vmem_limit_bytes=VMEM_LIMIT),
        name="merge_xattn",
    )(x2d, ya2d, of2d, ob2d, p2d, p2d, kvx, wa, wb, wo, wq, wox, ghg, gpm, gpx, gpox)


def _ffn_kernel(x_ref, g_ref, wg_ref, wu_ref, wd_ref, gpost_ref, o_ref, h_ref, *, nf):
    f = pl.program_id(1)

    @pl.when(f == 0)
    def _():
        h_ref[...] = _rms(x_ref[...], g_ref[...]).astype(BF16)
        o_ref[...] = jnp.zeros_like(o_ref)

    h = h_ref[...]
    gt = jnp.dot(h, wg_ref[...], preferred_element_type=F32)
    up = jnp.dot(h, wu_ref[...], preferred_element_type=F32)
    act = (gt * _sigmoid(gt) * up).astype(BF16)
    o_ref[...] += jnp.dot(act, wd_ref[...], preferred_element_type=F32)

    @pl.when(f == nf - 1)
    def _():
        o_ref[...] = x_ref[...] + _rms(o_ref[...], gpost_ref[...])


def _ffn(x2d, g, wgu, wd, gpost, tm, tf):
    t = x2d.shape[0]
    nf = D_FF // tf
    return pl.pallas_call(
        functools.partial(_ffn_kernel, nf=nf),
        grid=(t // tm, nf),
        in_specs=[
            pl.BlockSpec((tm, D_MODEL), lambda i, f: (i, 0)),
            pl.BlockSpec((1, D_MODEL), lambda i, f: (0, 0)),
            pl.BlockSpec((D_MODEL, tf), lambda i, f: (0, f)),
            pl.BlockSpec((D_MODEL, tf), lambda i, f: (0, nf + f)),
            pl.BlockSpec((tf, D_MODEL), lambda i, f: (f, 0)),
            pl.BlockSpec((1, D_MODEL), lambda i, f: (0, 0)),
        ],
        out_specs=pl.BlockSpec((tm, D_MODEL), lambda i, f: (i, 0)),
        out_shape=jax.ShapeDtypeStruct((t, D_MODEL), F32),
        scratch_shapes=[pltpu.VMEM((tm, D_MODEL), BF16)],
        compiler_params=pltpu.CompilerParams(
            dimension_semantics=("parallel", "arbitrary"), vmem_limit_bytes=VMEM_LIMIT),
        name="swiglu",
    )(x2d, g, wgu, wgu, wd, gpost)


def _rel_bucket(rel):
    nb = REL_BUCKETS // 2
    max_exact = nb // 2
    n = jnp.abs(rel)
    nf = jnp.maximum(n, 1).astype(jnp.float32)
    large = max_exact + (jnp.log(nf / max_exact) / math.log(REL_MAX_DIST / max_exact) * (nb - max_exact)).astype(jnp.int32)
    large = jnp.minimum(large, nb - 1)
    return jnp.where(rel > 0, nb, 0) + jnp.where(n < max_exact, n, large)


def _tiles(seq):
    tk = min(1024, seq)
    ns = 2
    tc = min(256, seq)
    tm_proj = min(1024, seq)
    tm_merge = min(256, seq)
    tm_ffn = min(512, seq)
    return tk, ns, tc, tm_proj, tm_merge, tm_ffn


def _layer(x, mem, w, seq_tiles):
    b, s, _ = x.shape
    tk, ns, tc, tm_proj, tm_merge, tm_ffn = seq_tiles
    x2d = x.reshape(b * s, D_MODEL)
    p2d = _inproj(x2d, w["g_pre_mix"], w["w_in"], w["colbias"], tm_proj)
    p3 = p2d.reshape(b, s, N_IN)
    ya = _attention(p3, w["attn_scal"], w["strip"][(tk, ns)], w["g_subln_col"], tk, ns)
    o_f, o_b = _hgrn(p3, w["lb2"], tc)
    kvx = _memkv(mem, w["g_mem"], w["w_kv_x"])
    x2 = _merge(x2d, ya.reshape(b * s, A_WIDTH), o_f.reshape(b * s, B_WIDTH), o_b.reshape(b * s, B_WIDTH),
                p2d, kvx, w["w_branch_a"], w["w_branch_b"], w["w_out"], w["w_q_x"], w["w_o_x"],
                w["g_hgrn_norm"], w["g_post_mix"], w["g_pre_x"], w["g_post_x"], s, tm_merge)
    y = _ffn(x2, w["g_pre_ffn"], w["w_gate_up"], w["w_down"], w["g_post_ffn"], tm_ffn, 512)
    return y.reshape(b, s, D_MODEL)


def _bias_strip(rel_bias, tk, tq):
    r = jnp.arange(-2 * tk, 2 * tk + tq, dtype=jnp.int32)[:, None]
    i = jnp.arange(tq, dtype=jnp.int32)[None, :]
    onehot = (_rel_bucket(r - i)[:, :, None] == jnp.arange(REL_BUCKETS, dtype=jnp.int32)).astype(F32)
    return jnp.einsum("rik,kh->hri", onehot, rel_bias.astype(F32) * LOG2E, precision=lax.Precision.HIGHEST)


def kernel(x_prompt, x_sample, mem_prompt, mem_sample, rel_bias, hgrn_lb_logits, g_pre_mix, w_in, b_merge,
           lam_q1, lam_k1, lam_q2, lam_k2, g_subln, g_hgrn_norm, w_branch_a, w_branch_b, w_out, g_post_mix,
           g_pre_x, g_mem, w_q_x, w_kv_x, w_o_x, g_post_x, g_pre_ffn, w_gate_up, w_down, g_post_ffn):
    l = 0
    row = lambda a: a[l].astype(F32).reshape(1, -1)
    lam = (jnp.exp(jnp.sum(lam_q1[l].astype(F32) * lam_k1[l].astype(F32)))
           - jnp.exp(jnp.sum(lam_q2[l].astype(F32) * lam_k2[l].astype(F32))) + LAM_INIT)
    lb_all = jnp.cumsum(jax.nn.softmax(hgrn_lb_logits.astype(F32), axis=1), axis=1)
    ones = jnp.ones((A_HEADS,), F32)
    attn_scal = jnp.stack([lam * ones, (1.0 - LAM_INIT) * ones], axis=1)
    tiles_p = _tiles(x_prompt.shape[1])
    tiles_s = _tiles(x_sample.shape[1])
    w = dict(
        g_pre_mix=row(g_pre_mix), w_in=w_in[l].astype(BF16),
        colbias=jnp.concatenate([jnp.zeros((1, N_IN - 2 * D_MODEL), F32), row(b_merge)], axis=1),
        attn_scal=attn_scal,
        strip={(tk, ns): _bias_strip(rel_bias, tk, ns * LANES) for tk, ns in {tiles_p[:2], tiles_s[:2]}},
        g_subln_col=jnp.broadcast_to(g_subln[l].astype(F32)[:, None], (A_VDIM, LANES)),
        lb2=jnp.stack([lb_all[0, l], lb_all[1, l]], axis=0),
        g_mem=row(g_mem), w_kv_x=w_kv_x[l].astype(BF16),
        w_branch_a=w_branch_a[l].astype(BF16), w_branch_b=w_branch_b[l].astype(BF16),
        w_out=w_out[l].astype(BF16), w_q_x=w_q_x[l].astype(BF16), w_o_x=w_o_x[l].astype(BF16),
        g_hgrn_norm=row(g_hgrn_norm), g_post_mix=row(g_post_mix), g_pre_x=row(g_pre_x), g_post_x=row(g_post_x),
        g_pre_ffn=row(g_pre_ffn), w_gate_up=w_gate_up[l].astype(BF16), w_down=w_down[l].astype(BF16),
        g_post_ffn=row(g_post_ffn),
    )
    return (_layer(x_prompt, mem_prompt, w, tiles_p), _layer(x_sample, mem_sample, w, tiles_s))
```

```python
import functools
import math

import jax
import jax.numpy as jnp
import numpy as np
from jax import lax
from jax.experimental import pallas as pl
from jax.experimental.pallas import tpu as pltpu

F32 = jnp.float32
BF16 = jnp.bfloat16

D_MODEL = 2048
A_WIDTH = 1024
A_HALF = 64
A_HEADS = 8
A_VDIM = 128
B_WIDTH = 1024
B_DIM = 128
B_HEADS = 8
CHUNK = 64
REL_BUCKETS = 32
REL_MAX_DIST = 128
X_HEADS = 4
X_DIM = 128
X_WIDTH = 512
D_FF = 5632
N_IN = 12288
EPS = 1e-6
LOG2E = 1.4426950408889634
LAM_INIT = 0.8 - 0.6 * math.exp(-0.3 * 0)

LANES = 128
VT_ROWS = A_VDIM + 16
ATTN_BUFS = 2
VMEM_LIMIT = 56 * 1024 * 1024

COL_QA, COL_KA, COL_VA, COL_QB, COL_ZF, COL_ZB, COL_VB, COL_GB, COL_GATE = range(9)

NT = (((1,), (1,)), ((), ()))
TN = (((0,), (0,)), ((), ()))


def _rms(x, g):
    ms = jnp.mean(x * x, axis=-1, keepdims=True)
    return x * lax.rsqrt(ms + EPS) * g


def _sigmoid(x):
    return 1.0 / (1.0 + jnp.exp(-x))


def _colmax(x):
    while x.shape[0] > 64:
        x = jnp.max(x.reshape(8, x.shape[0] // 8, x.shape[1]), axis=0)
    return jnp.max(x, axis=0, keepdims=True)


def _inproj_kernel(x_ref, g_ref, w_ref, cb_ref, o_ref, h_ref):
    j = pl.program_id(1)

    @pl.when(j == 0)
    def _():
        h_ref[...] = _rms(x_ref[...], g_ref[...]).astype(BF16)

    @pl.when(j < COL_GATE)
    def _():
        scale = jnp.where(j == COL_QA, A_HALF ** -0.5 * LOG2E, jnp.where(j == COL_QB, B_DIM ** -0.5, 1.0))
        acc = jnp.dot(h_ref[...], w_ref[...], preferred_element_type=F32)
        o_ref[...] = (acc * scale).astype(BF16)

    @pl.when(j >= COL_GATE)
    def _():
        acc = jnp.dot(h_ref[...], w_ref[...], preferred_element_type=F32)
        o_ref[...] = _sigmoid(acc + cb_ref[...]).astype(BF16)


def _inproj(x2d, g, w, colbias, tm):
    t = x2d.shape[0]
    tn = 1024
    return pl.pallas_call(
        _inproj_kernel,
        grid=(t // tm, N_IN // tn),
        in_specs=[
            pl.BlockSpec((tm, D_MODEL), lambda i, j: (i, 0)),
            pl.BlockSpec((1, D_MODEL), lambda i, j: (0, 0)),
            pl.BlockSpec((D_MODEL, tn), lambda i, j: (0, j)),
            pl.BlockSpec((1, tn), lambda i, j: (0, j)),
        ],
        out_specs=pl.BlockSpec((tm, tn), lambda i, j: (i, j)),
        out_shape=jax.ShapeDtypeStruct((t, N_IN), BF16),
        scratch_shapes=[pltpu.VMEM((tm, D_MODEL), BF16)],
        compiler_params=pltpu.CompilerParams(
            dimension_semantics=("parallel", "arbitrary"), vmem_limit_bytes=VMEM_LIMIT),
        name="inproj",
    )(x2d, g, w, colbias)


def _memkv_kernel(m_ref, g_ref, w_ref, o_ref):
    h = _rms(m_ref[0], g_ref[...]).astype(BF16)
    o_ref[0] = jnp.dot(h, w_ref[...], preferred_element_type=F32).astype(BF16)


def _memkv(mem, g, w):
    b, n, _ = mem.shape
    return pl.pallas_call(
        _memkv_kernel,
        grid=(b,),
        in_specs=[
            pl.BlockSpec((1, n, D_MODEL), lambda i: (i, 0, 0)),
            pl.BlockSpec((1, D_MODEL), lambda i: (0, 0)),
            pl.BlockSpec((D_MODEL, 2 * X_WIDTH), lambda i: (0, 0)),
        ],
        out_specs=pl.BlockSpec((1, n, 2 * X_WIDTH), lambda i: (i, 0, 0)),
        out_shape=jax.ShapeDtypeStruct((b, n, 2 * X_WIDTH), BF16),
        compiler_params=pltpu.CompilerParams(
            dimension_semantics=("parallel",), vmem_limit_bytes=VMEM_LIMIT),
        name="memkv",
    )(mem, g, w)


def _attn_kernel(sc_ref, q_ref, k_ref, v_ref, strip_ref, g_ref, o_ref, vt_ref, *scratch, seq, tk, ns):
    bufs, qst_ref, acc_ref = scratch[:-2], scratch[-2], scratch[-1]
    h = pl.program_id(1)
    tq = ns * LANES
    nk = seq // tk
    nt = (seq // tq) * nk
    r = tk // LANES
    lam = sc_ref[h, 0]
    post = sc_ref[h, 1]

    row = lax.broadcasted_iota(jnp.int32, (LANES, LANES), 0)
    lane = lax.broadcasted_iota(jnp.int32, (LANES, LANES), 1)
    eye = (row == lane).astype(BF16)

    ones_row = (lax.broadcasted_iota(jnp.int32, (VT_ROWS - A_VDIM, tk), 0) == 0).astype(BF16)

    def vt_body(j, carry):
        vc = v_ref[0, pl.ds(pl.multiple_of(j * tk, tk), tk), :]
        vt_ref[j, :A_VDIM, :] = lax.dot_general(eye, vc, NT, preferred_element_type=F32).astype(BF16)
        vt_ref[j, A_VDIM:, :] = ones_row
        return carry

    lax.fori_loop(0, nk, vt_body, 0)

    rb = tk
    nblk = tk // rb
    rows = lambda blk: slice(blk * rb, (blk + 1) * rb)

    def logits_setup(t):
        qi = t // nk
        j = t - qi * nk
        r0 = jnp.clip((j * r - qi * ns) * LANES, -2 * tk, tk + tq)
        for s in range(ns):
            qs = q_ref[0, pl.ds(pl.multiple_of(qi * tq + s * LANES, LANES), LANES), :]
            zero = jnp.zeros_like(qs)
            qst_ref[s] = jnp.concatenate([jnp.where(lane < A_HALF, qs, zero),
                                          jnp.where(lane >= A_HALF, qs, zero)], axis=0)
        return j * tk, r0 + 2 * tk

    def logits_block(ctx, blk, dst_ref, pmax):
        krow, brow = ctx
        kc = k_ref[0, pl.ds(pl.multiple_of(krow + blk * rb, rb), rb), :]
        bias = strip_ref[0, pl.ds(pl.multiple_of(brow + blk * rb, LANES), rb), :]
        out = []
        for s in range(ns):
            b = bias[:, s * LANES:(s + 1) * LANES]
            st = (lax.dot_general(kc, qst_ref[s], NT, preferred_element_type=F32)
                  + jnp.concatenate([b, b], axis=1))
            dst_ref[s, rows(blk), :] = st
            part = jnp.max(st.reshape(4, rb // 4, 2 * LANES), axis=0)
            part = jnp.max(part.reshape(rb // 32, 8, 2 * LANES), axis=0)
            out.append(part if pmax is None else jnp.maximum(pmax[s], part))
        return tuple(out)

    def softmax_setup(t, ms, maxima):
        first = t % nk == 0
        out = []
        for s in range(ns):
            m = jnp.where(first, -1e30, ms[s])
            m_new = jnp.maximum(m, maxima[s])
            out.append((m_new, jnp.exp2(m - m_new)))
        return tuple(out)

    def softmax_block(sm, blk, src_ref, dst_ref):
        for s in range(ns):
            dst_ref[s, rows(blk), :] = jnp.exp2(src_ref[s, rows(blk), :] - sm[s][0]).astype(BF16)

    def values_block(j, blk, p_ref, pv):
        vt = vt_ref[j, :, rows(blk)]
        out = []
        for s in range(ns):
            d = jnp.dot(vt, p_ref[s, rows(blk), :], preferred_element_type=F32)
            out.append(d if pv is None else pv[s] + d)
        return tuple(out)

    def colmax(pmax):
        return tuple(jnp.max(x, axis=0, keepdims=True) for x in pmax)

    def values_finish(t, pv, alphas, may_finish):
        qi = t // nk
        j = t - qi * nk
        for s in range(ns):
            acc_ref[s] = alphas[s] * acc_ref[s] + pv[s]

        if may_finish:
            @pl.when(j == nk - 1)
            def _():
                for s in range(ns):
                    acc = acc_ref[s]
                    o = acc[:A_VDIM] * (1.0 / acc[A_VDIM:A_VDIM + 1])
                    ot = o[:, :LANES] - lam * o[:, LANES:]
                    ms = jnp.mean(ot * ot, axis=0, keepdims=True)
                    y = ot * lax.rsqrt(ms + EPS) * g_ref[...] * post
                    o_ref[0, pl.ds(pl.multiple_of(qi * tq + s * LANES, LANES), LANES), :] = y.T.astype(BF16)

    s_refs = bufs[:ATTN_BUFS]
    p_refs = bufs[ATTN_BUFS:]

    def stage(t, k, carry, may_finish):
        ms, alphas, maxima = carry
        s_w, s_r = s_refs[(k + 2) % ATTN_BUFS], s_refs[(k + 1) % ATTN_BUFS]
        p_w, p_r = p_refs[(k + 1) % ATTN_BUFS], p_refs[k]
        ctx = logits_setup(jnp.minimum(t + 2, nt - 1))
        sm = softmax_setup(jnp.minimum(t + 1, nt - 1), ms, maxima)
        j = t % nk
        pmax = pv = None
        for blk in range(nblk):
            pmax = logits_block(ctx, blk, s_w, pmax)
            softmax_block(sm, blk, s_r, p_w)
            pv = values_block(j, blk, p_r, pv)
        values_finish(t, pv, alphas, may_finish)
        return tuple(x[0] for x in sm), tuple(x[1] for x in sm), colmax(pmax)

    def body(u, carry):
        for k in range(ATTN_BUFS):
            carry = stage(ATTN_BUFS * u + k, k, carry, may_finish=(nk - 1 - k) % math.gcd(ATTN_BUFS, nk) == 0)
        return carry

    def prologue_logits(t, dst_ref):
        ctx = logits_setup(t)
        pmax = None
        for blk in range(nblk):
            pmax = logits_block(ctx, blk, dst_ref, pmax)
        return colmax(pmax)

    acc_ref[...] = jnp.zeros_like(acc_ref)
    max0 = prologue_logits(0, s_refs[0])
    max1 = prologue_logits(1, s_refs[1])
    sm0 = softmax_setup(0, (jnp.full((1, 2 * LANES), -1e30, F32),) * ns, max0)
    for blk in range(nblk):
        softmax_block(sm0, blk, s_refs[0], p_refs[0])
    init = (tuple(x[0] for x in sm0), tuple(x[1] for x in sm0), max1)
    lax.fori_loop(0, nt // ATTN_BUFS, body, init)


def _attention(p3, scal, strip, gcol, tk, ns):
    b, s, _ = p3.shape
    kern = functools.partial(_attn_kernel, seq=s, tk=tk, ns=ns)
    return pl.pallas_call(
        kern,
        grid=(b, A_HEADS),
        in_specs=[
            pl.BlockSpec(memory_space=pltpu.SMEM),
            pl.BlockSpec((1, s, LANES), lambda i, h: (i, 0, COL_QA * 8 + h)),
            pl.BlockSpec((1, s, LANES), lambda i, h: (i, 0, COL_KA * 8 + h)),
            pl.BlockSpec((1, s, LANES), lambda i, h: (i, 0, COL_VA * 8 + h)),
            pl.BlockSpec((1, 4 * tk + ns * LANES, ns * LANES), lambda i, h: (h, 0, 0)),
            pl.BlockSpec((A_VDIM, LANES), lambda i, h: (0, 0)),
        ],
        out_specs=pl.BlockSpec((1, s, LANES), lambda i, h: (i, 0, h)),
        out_shape=jax.ShapeDtypeStruct((b, s, A_WIDTH), BF16),
        scratch_shapes=([pltpu.VMEM((s // tk, VT_ROWS, tk), BF16)]
                        + [pltpu.VMEM((ns, tk, 2 * LANES), F32)] * ATTN_BUFS
                        + [pltpu.VMEM((ns, tk, 2 * LANES), BF16)] * ATTN_BUFS
                        + [pltpu.VMEM((ns, 2 * LANES, LANES), BF16)]
                        + [pltpu.VMEM((ns, VT_ROWS, 2 * LANES), F32)]),
        compiler_params=pltpu.CompilerParams(
            dimension_semantics=("parallel", "arbitrary"), vmem_limit_bytes=VMEM_LIMIT),
        name="diffattn",
    )(scal, p3, p3, p3, strip, gcol)


def _cumsum_mm(tri, x):
    hi = x.astype(BF16)
    lo = (x - hi.astype(F32)).astype(BF16)
    d = lambda y: jnp.dot(tri, y, preferred_element_type=F32)
    return d(hi) + d(lo)


def _hgrn_chunk(q, z, v, c0, c1, tri, mask, ref_row, last_row, st_ref):
    f = c0 + c1 * jnp.tanh(0.5 * z)
    logf = jnp.log2(f)
    kk = 1.0 - f
    b = _cumsum_mm(tri, logf)
    b_ref = b[ref_row:ref_row + 1, :]
    b_last = b[last_row:last_row + 1, :]
    d = b - b_ref
    qd32 = q * jnp.exp2(d)
    kd32 = kk * jnp.exp2(-d)
    qd = qd32.astype(BF16)
    kd = kd32.astype(BF16)
    kl = (kd32 * jnp.exp2(b_last - b_ref)).astype(BF16)
    qe = (qd32 * jnp.exp2(b_ref)).astype(BF16)
    dec = jnp.exp2(b_last)
    vb = v
    outs = []
    for h in range(B_HEADS):
        sl = slice(h * B_DIM, (h + 1) * B_DIM)
        a = lax.dot_general(qd[:, sl], kd[:, sl], NT, preferred_element_type=F32)
        a = jnp.where(mask, a, 0.0).astype(BF16)
        intra = jnp.dot(a, vb[:, sl], preferred_element_type=F32)
        st = st_ref[h]
        inter = lax.dot_general(qe[:, sl], st.astype(BF16), NT, preferred_element_type=F32)
        outs.append(intra + inter)
        upd = lax.dot_general(vb[:, sl], kl[:, sl], TN, preferred_element_type=F32)
        st_ref[h] = st * dec[:, sl] + upd
    return jnp.concatenate(outs, axis=1)


def _hgrn_kernel(lb_ref, qf_ref, zf_ref, vf_ref, qb_ref, zb_ref, vb_ref, of_ref, ob_ref,
                 stf_ref, stb_ref, *, tc):
    @pl.when(pl.program_id(1) == 0)
    def _():
        stf_ref[...] = jnp.zeros_like(stf_ref)
        stb_ref[...] = jnp.zeros_like(stb_ref)

    nch = tc // CHUNK
    row = lax.broadcasted_iota(jnp.int32, (CHUNK, CHUNK), 0)
    col = lax.broadcasted_iota(jnp.int32, (CHUNK, CHUNK), 1)
    mask_f = col <= row
    mask_b = col >= row
    tri_f = mask_f.astype(BF16)
    tri_b = mask_b.astype(BF16)
    lb = lb_ref[...]
    c0 = 0.5 + 0.5 * lb
    c1 = 0.5 - 0.5 * lb

    def body(c, carry):
        rf = pl.ds(pl.multiple_of(c * CHUNK, CHUNK), CHUNK)
        rb = pl.ds(pl.multiple_of((nch - 1 - c) * CHUNK, CHUNK), CHUNK)
        ld = lambda ref, rows: ref[0, rows, :].astype(F32)
        of_ref[0, rf, :] = _hgrn_chunk(ld(qf_ref, rf), ld(zf_ref, rf), vf_ref[0, rf, :], c0[0:1], c1[0:1],
                                       tri_f, mask_f, CHUNK // 2 - 1, CHUNK - 1, stf_ref).astype(of_ref.dtype)
        ob_ref[0, rb, :] = _hgrn_chunk(ld(qb_ref, rb), ld(zb_ref, rb), vb_ref[0, rb, :], c0[1:2], c1[1:2],
                                       tri_b, mask_b, CHUNK // 2, 0, stb_ref).astype(ob_ref.dtype)
        return carry

    lax.fori_loop(0, nch, body, 0, unroll=2)


def _hgrn(p3, lb2, tc):
    b, s, _ = p3.shape
    nb = s // tc
    fwd = lambda col: pl.BlockSpec((1, tc, B_WIDTH), lambda i, t: (i, t, col))
    bwd = lambda col: pl.BlockSpec((1, tc, B_WIDTH), lambda i, t: (i, nb - 1 - t, col))
    return pl.pallas_call(
        functools.partial(_hgrn_kernel, tc=tc),
        grid=(b, nb),
        in_specs=[
            pl.BlockSpec((2, B_WIDTH), lambda i, t: (0, 0)),
            fwd(COL_QB), fwd(COL_ZF), fwd(COL_VB),
            bwd(COL_QB), bwd(COL_ZB), bwd(COL_VB),
        ],
        out_specs=[
            pl.BlockSpec((1, tc, B_WIDTH), lambda i, t: (i, t, 0)),
            pl.BlockSpec((1, tc, B_WIDTH), lambda i, t: (i, nb - 1 - t, 0)),
        ],
        out_shape=[jax.ShapeDtypeStruct((b, s, B_WIDTH), BF16)] * 2,
        scratch_shapes=[pltpu.VMEM((B_HEADS, B_DIM, B_DIM), F32)] * 2,
        compiler_params=pltpu.CompilerParams(
            dimension_semantics=("parallel", "arbitrary"), vmem_limit_bytes=VMEM_LIMIT),
        name="hgrn2",
    )(lb2, p3, p3, p3, p3, p3, p3)


def _merge_kernel(x_ref, ya_ref, of_ref, ob_ref, gb_ref, gate_ref, kv_ref,
                  wa_ref, wb_ref, wo_ref, wq_ref, wox_ref,
                  ghg_ref, gpm_ref, gpx_ref, gpox_ref, o_ref):
    o = of_ref[...].astype(F32) + ob_ref[...].astype(F32)
    gb = gb_ref[...].astype(F32)
    ys = []
    for h in range(B_HEADS):
        sl = slice(h * B_DIM, (h + 1) * B_DIM)
        gh = gb[:, sl]
        ys.append((_rms(o[:, sl], ghg_ref[...]) * (gh * _sigmoid(gh))).astype(BF16))
    yb = jnp.concatenate(ys, axis=1)
    br_a = jnp.dot(ya_ref[...], wa_ref[...], preferred_element_type=F32)
    br_b = jnp.dot(yb, wb_ref[...], preferred_element_type=F32)
    g = gate_ref[...].astype(F32)
    mix = g[:, :D_MODEL] * br_a + g[:, D_MODEL:] * br_b
    t = jnp.dot(mix.astype(BF16), wo_ref[...], preferred_element_type=F32)
    x1 = x_ref[...] + _rms(t, gpm_ref[...])

    h2 = _rms(x1, gpx_ref[...]).astype(BF16)
    qx = (jnp.dot(h2, wq_ref[...], preferred_element_type=F32) * (X_DIM ** -0.5 * LOG2E)).astype(BF16)
    kv = kv_ref[0]
    oxs = []
    for h in range(X_HEADS):
        sl = slice(h * X_DIM, (h + 1) * X_DIM)
        lg = lax.dot_general(qx[:, sl], kv[:, sl], NT, preferred_element_type=F32)
        p = jnp.exp2(lg - jnp.max(lg, axis=-1, keepdims=True))
        l = jnp.sum(p, axis=-1, keepdims=True)
        vh = kv[:, X_WIDTH + h * X_DIM:X_WIDTH + (h + 1) * X_DIM]
        oxs.append((jnp.dot(p.astype(BF16), vh, preferred_element_type=F32) * (1.0 / l)).astype(BF16))
    ox = jnp.concatenate(oxs, axis=1)
    t2 = jnp.dot(ox, wox_ref[...], preferred_element_type=F32)
    o_ref[...] = x1 + _rms(t2, gpox_ref[...])


def _merge(x2d, ya2d, of2d, ob2d, p2d, kvx, wa, wb, wo, wq, wox, ghg, gpm, gpx, gpox, seq, tm):
    t = x2d.shape[0]
    per_seq = seq // tm
    const = lambda shape: pl.BlockSpec(shape, lambda i: (0,) * len(shape), pipeline_mode=pl.Buffered(1))
    return pl.pallas_call(
        _merge_kernel,
        grid=(t // tm,),
        in_specs=[
            pl.BlockSpec((tm, D_MODEL), lambda i: (i, 0)),
            pl.BlockSpec((tm, A_WIDTH), lambda i: (i, 0)),
            pl.BlockSpec((tm, B_WIDTH), lambda i: (i, 0)),
            pl.BlockSpec((tm, B_WIDTH), lambda i: (i, 0)),
            pl.BlockSpec((tm, B_WIDTH), lambda i: (i, COL_GB)),
            pl.BlockSpec((tm, 2 * D_MODEL), lambda i: (i, COL_GATE // 4)),
            pl.BlockSpec((1,) + kvx.shape[1:], lambda i: (i // per_seq, 0, 0)),
            const((A_WIDTH, D_MODEL)), const((B_WIDTH, D_MODEL)), const((D_MODEL, D_MODEL)),
            const((D_MODEL, X_WIDTH)), const((X_WIDTH, D_MODEL)),
            const((1, B_DIM)), const((1, D_MODEL)), const((1, D_MODEL)), const((1, D_MODEL)),
        ],
        out_specs=pl.BlockSpec((tm, D_MODEL), lambda i: (i, 0)),
        out_shape=jax.ShapeDtypeStruct((t, D_MODEL), F32),
        compiler_params=pltpu.CompilerParams(
            dimension_semantics=("parallel",), vmem_limit_bytes=VMEM_LIMIT),
        name="merge_xattn",
    )(x2d, ya2d, of2d, ob2d, p2d, p2d, kvx, wa, wb, wo, wq, wox, ghg, gpm, gpx, gpox)


def _ffn_kernel(x_ref, g_ref, wg_ref, wu_ref, wd_ref, gpost_ref, o_ref, h_ref, *, nf):
    f = pl.program_id(1)

    @pl.when(f == 0)
    def _():
        h_ref[...] = _rms(x_ref[...], g_ref[...]).astype(BF16)
        o_ref[...] = jnp.zeros_like(o_ref)

    h = h_ref[...]
    gt = jnp.dot(h, wg_ref[...], preferred_element_type=F32)
    up = jnp.dot(h, wu_ref[...], preferred_element_type=F32)
    act = (gt * _sigmoid(gt) * up).astype(BF16)
    o_ref[...] += jnp.dot(act, wd_ref[...], preferred_element_type=F32)

    @pl.when(f == nf - 1)
    def _():
        o_ref[...] = x_ref[...] + _rms(o_ref[...], gpost_ref[...])


def _ffn(x2d, g, wgu, wd, gpost, tm, tf):
    t = x2d.shape[0]
    nf = D_FF // tf
    return pl.pallas_call(
        functools.partial(_ffn_kernel, nf=nf),
        grid=(t // tm, nf),
        in_specs=[
            pl.BlockSpec((tm, D_MODEL), lambda i, f: (i, 0)),
            pl.BlockSpec((1, D_MODEL), lambda i, f: (0, 0)),
            pl.BlockSpec((D_MODEL, tf), lambda i, f: (0, f)),
            pl.BlockSpec((D_MODEL, tf), lambda i, f: (0, nf + f)),
            pl.BlockSpec((tf, D_MODEL), lambda i, f: (f, 0)),
            pl.BlockSpec((1, D_MODEL), lambda i, f: (0, 0)),
        ],
        out_specs=pl.BlockSpec((tm, D_MODEL), lambda i, f: (i, 0)),
        out_shape=jax.ShapeDtypeStruct((t, D_MODEL), F32),
        scratch_shapes=[pltpu.VMEM((tm, D_MODEL), BF16)],
        compiler_params=pltpu.CompilerParams(
            dimension_semantics=("parallel", "arbitrary"), vmem_limit_bytes=VMEM_LIMIT),
        name="swiglu",
    )(x2d, g, wgu, wgu, wd, gpost)


def _rel_bucket(rel):
    nb = REL_BUCKETS // 2
    max_exact = nb // 2
    n = jnp.abs(rel)
    nf = jnp.maximum(n, 1).astype(jnp.float32)
    large = max_exact + (jnp.log(nf / max_exact) / math.log(REL_MAX_DIST / max_exact) * (nb - max_exact)).astype(jnp.int32)
    large = jnp.minimum(large, nb - 1)
    return jnp.where(rel > 0, nb, 0) + jnp.where(n < max_exact, n, large)


def _tiles(seq):
    tk = min(1024, seq)
    ns = 2
    tc = min(256, seq)
    tm_proj = min(1024, seq)
    tm_merge = min(256, seq)
    tm_ffn = min(512, seq)
    return tk, ns, tc, tm_proj, tm_merge, tm_ffn


def _layer(x, mem, w, seq_tiles):
    b, s, _ = x.shape
    tk, ns, tc, tm_proj, tm_merge, tm_ffn = seq_tiles
    x2d = x.reshape(b * s, D_MODEL)
    p2d = _inproj(x2d, w["g_pre_mix"], w["w_in"], w["colbias"], tm_proj)
    p3 = p2d.reshape(b, s, N_IN)
    ya = _attention(p3, w["attn_scal"], w["strip"][(tk, ns)], w["g_subln_col"], tk, ns)
    o_f, o_b = _hgrn(p3, w["lb2"], tc)
    kvx = _memkv(mem, w["g_mem"], w["w_kv_x"])
    x2 = _merge(x2d, ya.reshape(b * s, A_WIDTH), o_f.reshape(b * s, B_WIDTH), o_b.reshape(b * s, B_WIDTH),
                p2d, kvx, w["w_branch_a"], w["w_branch_b"], w["w_out"], w["w_q_x"], w["w_o_x"],
                w["g_hgrn_norm"], w["g_post_mix"], w["g_pre_x"], w["g_post_x"], s, tm_merge)
    y = _ffn(x2, w["g_pre_ffn"], w["w_gate_up"], w["w_down"], w["g_post_ffn"], tm_ffn, 512)
    return y.reshape(b, s, D_MODEL)


def _bias_strip(rel_bias, tk, tq):
    r = jnp.arange(-2 * tk, 2 * tk + tq, dtype=jnp.int32)[:, None]
    i = jnp.arange(tq, dtype=jnp.int32)[None, :]
    onehot = (_rel_bucket(r - i)[:, :, None] == jnp.arange(REL_BUCKETS, dtype=jnp.int32)).astype(F32)
    return jnp.einsum("rik,kh->hri", onehot, rel_bias.astype(F32) * LOG2E, precision=lax.Precision.HIGHEST)


def kernel(x_prompt, x_sample, mem_prompt, mem_sample, rel_bias, hgrn_lb_logits, g_pre_mix, w_in, b_merge,
           lam_q1, lam_k1, lam_q2, lam_k2, g_subln, g_hgrn_norm, w_branch_a, w_branch_b, w_out, g_post_mix,
           g_pre_x, g_mem, w_q_x, w_kv_x, w_o_x, g_post_x, g_pre_ffn, w_gate_up, w_down, g_post_ffn):
    l = 0
    row = lambda a: a[l].astype(F32).reshape(1, -1)
    lam = (jnp.exp(jnp.sum(lam_q1[l].astype(F32) * lam_k1[l].astype(F32)))
           - jnp.exp(jnp.sum(lam_q2[l].astype(F32) * lam_k2[l].astype(F32))) + LAM_INIT)
    lb_all = jnp.cumsum(jax.nn.softmax(hgrn_lb_logits.astype(F32), axis=1), axis=1)
    ones = jnp.ones((A_HEADS,), F32)
    attn_scal = jnp.stack([lam * ones, (1.0 - LAM_INIT) * ones], axis=1)
    tiles_p = _tiles(x_prompt.shape[1])
    tiles_s = _tiles(x_sample.shape[1])
    w = dict(
        g_pre_mix=row(g_pre_mix), w_in=w_in[l].astype(BF16),
        colbias=jnp.concatenate([jnp.zeros((1, N_IN - 2 * D_MODEL), F32), row(b_merge)], axis=1),
        attn_scal=attn_scal,
        strip={(tk, ns): _bias_strip(rel_bias, tk, ns * LANES) for tk, ns in {tiles_p[:2], tiles_s[:2]}},
        g_subln_col=jnp.broadcast_to(g_subln[l].astype(F32)[:, None], (A_VDIM, LANES)),
        lb2=jnp.stack([lb_all[0, l], lb_all[1, l]], axis=0),
        g_mem=row(g_mem), w_kv_x=w_kv_x[l].astype(BF16),
        w_branch_a=w_branch_a[l].astype(BF16), w_branch_b=w_branch_b[l].astype(BF16),
        w_out=w_out[l].astype(BF16), w_q_x=w_q_x[l].astype(BF16), w_o_x=w_o_x[l].astype(BF16),
        g_hgrn_norm=row(g_hgrn_norm), g_post_mix=row(g_post_mix), g_pre_x=row(g_pre_x), g_post_x=row(g_post_x),
        g_pre_ffn=row(g_pre_ffn), w_gate_up=w_gate_up[l].astype(BF16), w_down=w_down[l].astype(BF16),
        g_post_ffn=row(g_post_ffn),
    )
    return (_layer(x_prompt, mem_prompt, w, tiles_p), _layer(x_sample, mem_sample, w, tiles_s))
```

```python
import functools
import math

import jax
import jax.numpy as jnp
import numpy as np
from jax import lax
from jax.experimental import pallas as pl
from jax.experimental.pallas import tpu as pltpu

F32 = jnp.float32
BF16 = jnp.bfloat16

D_MODEL = 2048
A_WIDTH = 1024
A_HALF = 64
A_HEADS = 8
A_VDIM = 128
B_WIDTH = 1024
B_DIM = 128
B_HEADS = 8
CHUNK = 64
REL_BUCKETS = 32
REL_MAX_DIST = 128
X_HEADS = 4
X_DIM = 128
X_WIDTH = 512
D_FF = 5632
N_IN = 12288
EPS = 1e-6
LOG2E = 1.4426950408889634
LAM_INIT = 0.8 - 0.6 * math.exp(-0.3 * 0)

LANES = 128
VT_ROWS = A_VDIM + 16
ATTN_BUFS = 2
VMEM_LIMIT = 56 * 1024 * 1024

COL_QA, COL_KA, COL_VA, COL_QB, COL_ZF, COL_ZB, COL_VB, COL_GB, COL_GATE = range(9)

NT = (((1,), (1,)), ((), ()))
TN = (((0,), (0,)), ((), ()))


def _rms(x, g):
    ms = jnp.mean(x * x, axis=-1, keepdims=True)
    return x * lax.rsqrt(ms + EPS) * g


def _sigmoid(x):
    return 1.0 / (1.0 + jnp.exp(-x))


def _inproj_kernel(x_ref, g_ref, w_ref, cb_ref, o_ref, h_ref):
    j = pl.program_id(1)

    @pl.when(j == 0)
    def _():
        h_ref[...] = _rms(x_ref[...], g_ref[...]).astype(BF16)

    @pl.when(j < COL_GATE)
    def _():
        scale = jnp.where(j == COL_QA, A_HALF ** -0.5 * LOG2E, jnp.where(j == COL_QB, B_DIM ** -0.5, 1.0))
        acc = jnp.dot(h_ref[...], w_ref[...], preferred_element_type=F32)
        o_ref[...] = (acc * scale).astype(BF16)

    @pl.when(j >= COL_GATE)
    def _():
        acc = jnp.dot(h_ref[...], w_ref[...], preferred_element_type=F32)
        o_ref[...] = _sigmoid(acc + cb_ref[...]).astype(BF16)


def _inproj(x2d, g, w, colbias, tm):
    t = x2d.shape[0]
    tn = 1024
    return pl.pallas_call(
        _inproj_kernel,
        grid=(t // tm, N_IN // tn),
        in_specs=[
            pl.BlockSpec((tm, D_MODEL), lambda i, j: (i, 0)),
            pl.BlockSpec((1, D_MODEL), lambda i, j: (0, 0)),
            pl.BlockSpec((D_MODEL, tn), lambda i, j: (0, j)),
            pl.BlockSpec((1, tn), lambda i, j: (0, j)),
        ],
        out_specs=pl.BlockSpec((tm, tn), lambda i, j: (i, j)),
        out_shape=jax.ShapeDtypeStruct((t, N_IN), BF16),
        scratch_shapes=[pltpu.VMEM((tm, D_MODEL), BF16)],
        compiler_params=pltpu.CompilerParams(
            dimension_semantics=("parallel", "arbitrary"), vmem_limit_bytes=VMEM_LIMIT),
        name="inproj",
    )(x2d, g, w, colbias)


def _memkv_kernel(m_ref, g_ref, w_ref, o_ref):
    h = _rms(m_ref[0], g_ref[...]).astype(BF16)
    o_ref[0] = jnp.dot(h, w_ref[...], preferred_element_type=F32).astype(BF16)


def _memkv(mem, g, w):
    b, n, _ = mem.shape
    return pl.pallas_call(
        _memkv_kernel,
        grid=(b,),
        in_specs=[
            pl.BlockSpec((1, n, D_MODEL), lambda i: (i, 0, 0)),
            pl.BlockSpec((1, D_MODEL), lambda i: (0, 0)),
            pl.BlockSpec((D_MODEL, 2 * X_WIDTH), lambda i: (0, 0)),
        ],
        out_specs=pl.BlockSpec((1, n, 2 * X_WIDTH), lambda i: (i, 0, 0)),
        out_shape=jax.ShapeDtypeStruct((b, n, 2 * X_WIDTH), BF16),
        compiler_params=pltpu.CompilerParams(
            dimension_semantics=("parallel",), vmem_limit_bytes=VMEM_LIMIT),
        name="memkv",
    )(mem, g, w)


def _attn_kernel(sc_ref, q_ref, k_ref, v_ref, strip_ref, g_ref, o_ref, vt_ref, *scratch, seq, tk, ns):
    bufs, qst_ref, acc_ref = scratch[:-2], scratch[-2], scratch[-1]
    h = pl.program_id(1)
    tq = ns * LANES
    nk = seq // tk
    nt = (seq // tq) * nk
    r = tk // LANES
    lam = sc_ref[h, 0]
    post = sc_ref[h, 1]

    row = lax.broadcasted_iota(jnp.int32, (LANES, LANES), 0)
    lane = lax.broadcasted_iota(jnp.int32, (LANES, LANES), 1)
    eye = (row == lane).astype(BF16)

    ones_row = (lax.broadcasted_iota(jnp.int32, (VT_ROWS - A_VDIM, tk), 0) == 0).astype(BF16)

    def vt_body(j, carry):
        vc = v_ref[0, pl.ds(pl.multiple_of(j * tk, tk), tk), :]
        vt_ref[j, :A_VDIM, :] = lax.dot_general(eye, vc, NT, preferred_element_type=F32).astype(BF16)
        vt_ref[j, A_VDIM:, :] = ones_row
        return carry

    lax.fori_loop(0, nk, vt_body, 0)

    def logits(t, dst_ref):
        qi = t // nk
        j = t - qi * nk
        r0 = jnp.clip((j * r - qi * ns) * LANES, -2 * tk, tk + tq)
        kc = k_ref[0, pl.ds(pl.multiple_of(j * tk, tk), tk), :]
        bias = strip_ref[0, pl.ds(pl.multiple_of(r0 + 2 * tk, LANES), tk), :]
        out = []
        for s in range(ns):
            qs = q_ref[0, pl.ds(pl.multiple_of(qi * tq + s * LANES, LANES), LANES), :]
            zero = jnp.zeros_like(qs)
            qst_ref[s] = jnp.concatenate([jnp.where(lane < A_HALF, qs, zero),
                                          jnp.where(lane >= A_HALF, qs, zero)], axis=0)
            b = bias[:, s * LANES:(s + 1) * LANES]
            st = (lax.dot_general(kc, qst_ref[s], NT, preferred_element_type=F32)
                  + jnp.concatenate([b, b], axis=1))
            dst_ref[s] = st
            part = jnp.max(st.reshape(4, tk // 4, 2 * LANES), axis=0)
            part = jnp.max(part.reshape(tk // 32, 8, 2 * LANES), axis=0)
            out.append(jnp.max(part, axis=0, keepdims=True))
        return tuple(out)

    def softmax(t, src_ref, dst_ref, ms, maxima):
        first = t % nk == 0
        out = []
        for s in range(ns):
            m = jnp.where(first, -1e30, ms[s])
            m_new = jnp.maximum(m, maxima[s])
            dst_ref[s] = jnp.exp2(src_ref[s] - m_new).astype(BF16)
            out.append((m_new, jnp.exp2(m - m_new)))
        return tuple(out)

    def values(t, p_ref, alphas, may_finish):
        qi = t // nk
        j = t - qi * nk
        vt = vt_ref[j]
        for s in range(ns):
            acc_ref[s] = alphas[s] * acc_ref[s] + jnp.dot(vt, p_ref[s], preferred_element_type=F32)

        if may_finish:
            @pl.when(j == nk - 1)
            def _():
                for s in range(ns):
                    acc = acc_ref[s]
                    o = acc[:A_VDIM] * (1.0 / acc[A_VDIM:A_VDIM + 1])
                    ot = o[:, :LANES] - lam * o[:, LANES:]
                    msq = jnp.mean(ot * ot, axis=0, keepdims=True)
                    y = ot * lax.rsqrt(msq + EPS) * g_ref[...] * post
                    o_ref[0, pl.ds(pl.multiple_of(qi * tq + s * LANES, LANES), LANES), :] = y.T.astype(BF16)

    s_refs = bufs[:ATTN_BUFS]
    p_refs = bufs[ATTN_BUFS:]

    def stage(t, k, carry, may_finish):
        ms, alphas, maxima = carry
        nxt_max = logits(jnp.minimum(t + 2, nt - 1), s_refs[(k + 2) % ATTN_BUFS])
        sm = softmax(jnp.minimum(t + 1, nt - 1), s_refs[(k + 1) % ATTN_BUFS], p_refs[(k + 1) % ATTN_BUFS],
                     ms, maxima)
        values(t, p_refs[k], alphas, may_finish)
        return tuple(x[0] for x in sm), tuple(x[1] for x in sm), nxt_max

    def body(u, carry):
        for k in range(ATTN_BUFS):
            carry = stage(ATTN_BUFS * u + k, k, carry, may_finish=(nk - 1 - k) % math.gcd(ATTN_BUFS, nk) == 0)
        return carry

    acc_ref[...] = jnp.zeros_like(acc_ref)
    max0 = logits(0, s_refs[0])
    max1 = logits(1, s_refs[1])
    sm0 = softmax(0, s_refs[0], p_refs[0], (jnp.full((1, 2 * LANES), -1e30, F32),) * ns, max0)
    init = (tuple(x[0] for x in sm0), tuple(x[1] for x in sm0), max1)
    lax.fori_loop(0, nt // ATTN_BUFS, body, init)


def _attention(p3, scal, strip, gcol, tk, ns):
    b, s, _ = p3.shape
    assert (s // (ns * LANES)) * (s // tk) % ATTN_BUFS == 0
    kern = functools.partial(_attn_kernel, seq=s, tk=tk, ns=ns)
    return pl.pallas_call(
        kern,
        grid=(b, A_HEADS),
        in_specs=[
            pl.BlockSpec(memory_space=pltpu.SMEM),
            pl.BlockSpec((1, s, LANES), lambda i, h: (i, 0, COL_QA * 8 + h)),
            pl.BlockSpec((1, s, LANES), lambda i, h: (i, 0, COL_KA * 8 + h)),
            pl.BlockSpec((1, s, LANES), lambda i, h: (i, 0, COL_VA * 8 + h)),
            pl.BlockSpec((1, 4 * tk + ns * LANES, ns * LANES), lambda i, h: (h, 0, 0)),
            pl.BlockSpec((A_VDIM, LANES), lambda i, h: (0, 0)),
        ],
        out_specs=pl.BlockSpec((1, s, LANES), lambda i, h: (i, 0, h)),
        out_shape=jax.ShapeDtypeStruct((b, s, A_WIDTH), BF16),
        scratch_shapes=([pltpu.VMEM((s // tk, VT_ROWS, tk), BF16)]
                        + [pltpu.VMEM((ns, tk, 2 * LANES), F32)] * ATTN_BUFS
                        + [pltpu.VMEM((ns, tk, 2 * LANES), BF16)] * ATTN_BUFS
                        + [pltpu.VMEM((ns, 2 * LANES, LANES), BF16)]
                        + [pltpu.VMEM((ns, VT_ROWS, 2 * LANES), F32)]),
        compiler_params=pltpu.CompilerParams(
            dimension_semantics=("parallel", "arbitrary"), vmem_limit_bytes=VMEM_LIMIT),
        name="diffattn",
    )(scal, p3, p3, p3, strip, gcol)


def _cumsum_mm(tri, x):
    hi = x.astype(BF16)
    lo = (x - hi.astype(F32)).astype(BF16)
    d = lambda y: jnp.dot(tri, y, preferred_element_type=F32)
    return d(hi) + d(lo)


def _hgrn_gates(q, z, c0, c1, tri, ref_row, last_row):
    f = c0 + c1 * jnp.tanh(0.5 * z)
    logf = jnp.log2(f)
    kk = 1.0 - f
    b = _cumsum_mm(tri, logf)
    b_ref = b[ref_row:ref_row + 1, :]
    b_last = b[last_row:last_row + 1, :]
    d = b - b_ref
    qd32 = q * jnp.exp2(d)
    kd32 = kk * jnp.exp2(-d)
    kl = (kd32 * jnp.exp2(b_last - b_ref)).astype(BF16)
    qe = (qd32 * jnp.exp2(b_ref)).astype(BF16)
    return qd32.astype(BF16), kd32.astype(BF16), kl, qe, jnp.exp2(b_last)


def _hgrn_heads(gates, vs, masks, st_refs):
    heads = [slice(h * B_DIM, (h + 1) * B_DIM) for h in range(B_HEADS)]
    n = len(gates)
    a = [[lax.dot_general(gates[i][0][:, sl], gates[i][1][:, sl], NT, preferred_element_type=F32) for sl in heads]
         for i in range(n)]
    st = [[st_refs[i][h] for h in range(B_HEADS)] for i in range(n)]
    inter = [[lax.dot_general(gates[i][3][:, sl], st[i][h].astype(BF16), NT, preferred_element_type=F32)
              for h, sl in enumerate(heads)] for i in range(n)]
    for i in range(n):
        for h, sl in enumerate(heads):
            upd = lax.dot_general(vs[i][:, sl], gates[i][2][:, sl], TN, preferred_element_type=F32)
            st_refs[i][h] = st[i][h] * gates[i][4][:, sl] + upd
    outs = []
    for i in range(n):
        o = [jnp.dot(jnp.where(masks[i], a[i][h], 0.0).astype(BF16), vs[i][:, sl], preferred_element_type=F32)
             + inter[i][h] for h, sl in enumerate(heads)]
        outs.append(jnp.concatenate(o, axis=1))
    return outs


def _hgrn_kernel(lb_ref, qf_ref, zf_ref, vf_ref, qb_ref, zb_ref, vb_ref, of_ref, ob_ref,
                 stf_ref, stb_ref, *, tc):
    @pl.when(pl.program_id(1) == 0)
    def _():
        stf_ref[...] = jnp.zeros_like(stf_ref)
        stb_ref[...] = jnp.zeros_like(stb_ref)

    nch = tc // CHUNK
    row = lax.broadcasted_iota(jnp.int32, (CHUNK, CHUNK), 0)
    col = lax.broadcasted_iota(jnp.int32, (CHUNK, CHUNK), 1)
    mask_f = col <= row
    mask_b = col >= row
    tri_f = mask_f.astype(BF16)
    tri_b = mask_b.astype(BF16)
    lb = lb_ref[...]
    c0 = 0.5 + 0.5 * lb
    c1 = 0.5 - 0.5 * lb

    def body(c, carry):
        rf = pl.ds(pl.multiple_of(c * CHUNK, CHUNK), CHUNK)
        rb = pl.ds(pl.multiple_of((nch - 1 - c) * CHUNK, CHUNK), CHUNK)
        ld = lambda ref, rows: ref[0, rows, :].astype(F32)
        gates_f = _hgrn_gates(ld(qf_ref, rf), ld(zf_ref, rf), c0[0:1], c1[0:1], tri_f, CHUNK // 2 - 1, CHUNK - 1)
        gates_b = _hgrn_gates(ld(qb_ref, rb), ld(zb_ref, rb), c0[1:2], c1[1:2], tri_b, CHUNK // 2, 0)
        out_f, out_b = _hgrn_heads((gates_f, gates_b), (vf_ref[0, rf, :], vb_ref[0, rb, :]),
                                   (mask_f, mask_b), (stf_ref, stb_ref))
        of_ref[0, rf, :] = out_f.astype(of_ref.dtype)
        ob_ref[0, rb, :] = out_b.astype(ob_ref.dtype)
        return carry

    lax.fori_loop(0, nch, body, 0, unroll=2)


def _hgrn(p3, lb2, tc):
    b, s, _ = p3.shape
    nb = s // tc
    fwd = lambda col: pl.BlockSpec((1, tc, B_WIDTH), lambda i, t: (i, t, col))
    bwd = lambda col: pl.BlockSpec((1, tc, B_WIDTH), lambda i, t: (i, nb - 1 - t, col))
    return pl.pallas_call(
        functools.partial(_hgrn_kernel, tc=tc),
        grid=(b, nb),
        in_specs=[
            pl.BlockSpec((2, B_WIDTH), lambda i, t: (0, 0)),
            fwd(COL_QB), fwd(COL_ZF), fwd(COL_VB),
            bwd(COL_QB), bwd(COL_ZB), bwd(COL_VB),
        ],
        out_specs=[
            pl.BlockSpec((1, tc, B_WIDTH), lambda i, t: (i, t, 0)),
            pl.BlockSpec((1, tc, B_WIDTH), lambda i, t: (i, nb - 1 - t, 0)),
        ],
        out_shape=[jax.ShapeDtypeStruct((b, s, B_WIDTH), BF16)] * 2,
        scratch_shapes=[pltpu.VMEM((B_HEADS, B_DIM, B_DIM), F32)] * 2,
        compiler_params=pltpu.CompilerParams(
            dimension_semantics=("parallel", "arbitrary"), vmem_limit_bytes=VMEM_LIMIT),
        name="hgrn2",
    )(lb2, p3, p3, p3, p3, p3, p3)


def _merge_kernel(x_ref, ya_ref, of_ref, ob_ref, gb_ref, gate_ref, kv_ref,
                  wa_ref, wb_ref, wo_ref, wq_ref, wox_ref,
                  ghg_ref, gpm_ref, gpx_ref, gpox_ref, o_ref):
    o = of_ref[...].astype(F32) + ob_ref[...].astype(F32)
    gb = gb_ref[...].astype(F32)
    ys = []
    for h in range(B_HEADS):
        sl = slice(h * B_DIM, (h + 1) * B_DIM)
        gh = gb[:, sl]
        ys.append((_rms(o[:, sl], ghg_ref[...]) * (gh * _sigmoid(gh))).astype(BF16))
    yb = jnp.concatenate(ys, axis=1)
    br_a = jnp.dot(ya_ref[...], wa_ref[...], preferred_element_type=F32)
    br_b = jnp.dot(yb, wb_ref[...], preferred_element_type=F32)
    g = gate_ref[...].astype(F32)
    mix = g[:, :D_MODEL] * br_a + g[:, D_MODEL:] * br_b
    t = jnp.dot(mix.astype(BF16), wo_ref[...], preferred_element_type=F32)
    x1 = x_ref[...] + _rms(t, gpm_ref[...])

    h2 = _rms(x1, gpx_ref[...]).astype(BF16)
    qx = (jnp.dot(h2, wq_ref[...], preferred_element_type=F32) * (X_DIM ** -0.5 * LOG2E)).astype(BF16)
    kv = kv_ref[0]
    heads = [slice(h * X_DIM, (h + 1) * X_DIM) for h in range(X_HEADS)]
    lgs = [lax.dot_general(qx[:, sl], kv[:, sl], NT, preferred_element_type=F32) for sl in heads]
    oxs = []
    for h, lg in enumerate(lgs):
        p = jnp.exp2(lg - jnp.max(lg, axis=-1, keepdims=True))
        l = jnp.sum(p, axis=-1, keepdims=True)
        vh = kv[:, X_WIDTH + h * X_DIM:X_WIDTH + (h + 1) * X_DIM]
        oxs.append((jnp.dot(p.astype(BF16), vh, preferred_element_type=F32) * (1.0 / l)).astype(BF16))
    ox = jnp.concatenate(oxs, axis=1)
    t2 = jnp.dot(ox, wox_ref[...], preferred_element_type=F32)
    o_ref[...] = x1 + _rms(t2, gpox_ref[...])


def _merge(x2d, ya2d, of2d, ob2d, p2d, kvx, wa, wb, wo, wq, wox, ghg, gpm, gpx, gpox, seq, tm):
    t = x2d.shape[0]
    per_seq = seq // tm
    const = lambda shape: pl.BlockSpec(shape, lambda i: (0,) * len(shape), pipeline_mode=pl.Buffered(1))
    return pl.pallas_call(
        _merge_kernel,
        grid=(t // tm,),
        in_specs=[
            pl.BlockSpec((tm, D_MODEL), lambda i: (i, 0)),
            pl.BlockSpec((tm, A_WIDTH), lambda i: (i, 0)),
            pl.BlockSpec((tm, B_WIDTH), lambda i: (i, 0)),
            pl.BlockSpec((tm, B_WIDTH), lambda i: (i, 0)),
            pl.BlockSpec((tm, B_WIDTH), lambda i: (i, COL_GB)),
            pl.BlockSpec((tm, 2 * D_MODEL), lambda i: (i, COL_GATE // 4)),
            pl.BlockSpec((1,) + kvx.shape[1:], lambda i: (i // per_seq, 0, 0)),
            const((A_WIDTH, D_MODEL)), const((B_WIDTH, D_MODEL)), const((D_MODEL, D_MODEL)),
            const((D_MODEL, X_WIDTH)), const((X_WIDTH, D_MODEL)),
            const((1, B_DIM)), const((1, D_MODEL)), const((1, D_MODEL)), const((1, D_MODEL)),
        ],
        out_specs=pl.BlockSpec((tm, D_MODEL), lambda i: (i, 0)),
        out_shape=jax.ShapeDtypeStruct((t, D_MODEL), F32),
        compiler_params=pltpu.CompilerParams(
            dimension_semantics=("parallel",), vmem_limit_bytes=VMEM_LIMIT),
        name="merge_xattn",
    )(x2d, ya2d, of2d, ob2d, p2d, p2d, kvx, wa, wb, wo, wq, wox, ghg, gpm, gpx, gpox)


def _ffn_kernel(x_ref, g_ref, wg_ref, wu_ref, wd_ref, gpost_ref, o_ref, h_ref, *, nf):
    f = pl.program_id(1)

    @pl.when(f == 0)
    def _():
        h_ref[...] = _rms(x_ref[...], g_ref[...]).astype(BF16)
        o_ref[...] = jnp.zeros_like(o_ref)

    h = h_ref[...]
    gt = jnp.dot(h, wg_ref[...], preferred_element_type=F32)
    up = jnp.dot(h, wu_ref[...], preferred_element_type=F32)
    act = (gt * _sigmoid(gt) * up).astype(BF16)
    o_ref[...] += jnp.dot(act, wd_ref[...], preferred_element_type=F32)

    @pl.when(f == nf - 1)
    def _():
        o_ref[...] = x_ref[...] + _rms(o_ref[...], gpost_ref[...])


def _ffn(x2d, g, wgu, wd, gpost, tm, tf):
    t = x2d.shape[0]
    nf = D_FF // tf
    return pl.pallas_call(
        functools.partial(_ffn_kernel, nf=nf),
        grid=(t // tm, nf),
        in_specs=[
            pl.BlockSpec((tm, D_MODEL), lambda i, f: (i, 0)),
            pl.BlockSpec((1, D_MODEL), lambda i, f: (0, 0)),
            pl.BlockSpec((D_MODEL, tf), lambda i, f: (0, f)),
            pl.BlockSpec((D_MODEL, tf), lambda i, f: (0, nf + f)),
            pl.BlockSpec((tf, D_MODEL), lambda i, f: (f, 0)),
            pl.BlockSpec((1, D_MODEL), lambda i, f: (0, 0)),
        ],
        out_specs=pl.BlockSpec((tm, D_MODEL), lambda i, f: (i, 0)),
        out_shape=jax.ShapeDtypeStruct((t, D_MODEL), F32),
        scratch_shapes=[pltpu.VMEM((tm, D_MODEL), BF16)],
        compiler_params=pltpu.CompilerParams(
            dimension_semantics=("parallel", "arbitrary"), vmem_limit_bytes=VMEM_LIMIT),
        name="swiglu",
    )(x2d, g, wgu, wgu, wd, gpost)


def _rel_bucket(rel):
    nb = REL_BUCKETS // 2
    max_exact = nb // 2
    n = jnp.abs(rel)
    nf = jnp.maximum(n, 1).astype(jnp.float32)
    large = max_exact + (jnp.log(nf / max_exact) / math.log(REL_MAX_DIST / max_exact) * (nb - max_exact)).astype(jnp.int32)
    large = jnp.minimum(large, nb - 1)
    return jnp.where(rel > 0, nb, 0) + jnp.where(n < max_exact, n, large)


def _tiles(seq):
    tk = min(1024, seq)
    ns = 2
    tc = min(256, seq)
    tm_proj = min(1024, seq)
    tm_merge = min(256, seq)
    tm_ffn = min(512, seq)
    return tk, ns, tc, tm_proj, tm_merge, tm_ffn


def _layer(x, mem, w, seq_tiles):
    b, s, _ = x.shape
    tk, ns, tc, tm_proj, tm_merge, tm_ffn = seq_tiles
    x2d = x.reshape(b * s, D_MODEL)
    p2d = _inproj(x2d, w["g_pre_mix"], w["w_in"], w["colbias"], tm_proj)
    p3 = p2d.reshape(b, s, N_IN)
    ya = _attention(p3, w["attn_scal"], w["strip"][(tk, ns)], w["g_subln_col"], tk, ns)
    o_f, o_b = _hgrn(p3, w["lb2"], tc)
    kvx = _memkv(mem, w["g_mem"], w["w_kv_x"])
    x2 = _merge(x2d, ya.reshape(b * s, A_WIDTH), o_f.reshape(b * s, B_WIDTH), o_b.reshape(b * s, B_WIDTH),
                p2d, kvx, w["w_branch_a"], w["w_branch_b"], w["w_out"], w["w_q_x"], w["w_o_x"],
                w["g_hgrn_norm"], w["g_post_mix"], w["g_pre_x"], w["g_post_x"], s, tm_merge)
    y = _ffn(x2, w["g_pre_ffn"], w["w_gate_up"], w["w_down"], w["g_post_ffn"], tm_ffn, 512)
    return y.reshape(b, s, D_MODEL)


def _bias_strip(rel_bias, tk, tq):
    r = jnp.arange(-2 * tk, 2 * tk + tq, dtype=jnp.int32)[:, None]
    i = jnp.arange(tq, dtype=jnp.int32)[None, :]
    onehot = (_rel_bucket(r - i)[:, :, None] == jnp.arange(REL_BUCKETS, dtype=jnp.int32)).astype(F32)
    return jnp.einsum("rik,kh->hri", onehot, rel_bias.astype(F32) * LOG2E, precision=lax.Precision.HIGHEST)


def kernel(x_prompt, x_sample, mem_prompt, mem_sample, rel_bias, hgrn_lb_logits, g_pre_mix, w_in, b_merge,
           lam_q1, lam_k1, lam_q2, lam_k2, g_subln, g_hgrn_norm, w_branch_a, w_branch_b, w_out, g_post_mix,
           g_pre_x, g_mem, w_q_x, w_kv_x, w_o_x, g_post_x, g_pre_ffn, w_gate_up, w_down, g_post_ffn):
    l = 0
    row = lambda a: a[l].astype(F32).reshape(1, -1)
    lam = (jnp.exp(jnp.sum(lam_q1[l].astype(F32) * lam_k1[l].astype(F32)))
           - jnp.exp(jnp.sum(lam_q2[l].astype(F32) * lam_k2[l].astype(F32))) + LAM_INIT)
    lb_all = jnp.cumsum(jax.nn.softmax(hgrn_lb_logits.astype(F32), axis=1), axis=1)
    ones = jnp.ones((A_HEADS,), F32)
    attn_scal = jnp.stack([lam * ones, (1.0 - LAM_INIT) * ones], axis=1)
    tiles_p = _tiles(x_prompt.shape[1])
    tiles_s = _tiles(x_sample.shape[1])
    w = dict(
        g_pre_mix=row(g_pre_mix), w_in=w_in[l].astype(BF16),
        colbias=jnp.concatenate([jnp.zeros((1, N_IN - 2 * D_MODEL), F32), row(b_merge)], axis=1),
        attn_scal=attn_scal,
        strip={(tk, ns): _bias_strip(rel_bias, tk, ns * LANES) for tk, ns in {tiles_p[:2], tiles_s[:2]}},
        g_subln_col=jnp.broadcast_to(g_subln[l].astype(F32)[:, None], (A_VDIM, LANES)),
        lb2=jnp.stack([lb_all[0, l], lb_all[1, l]], axis=0),
        g_mem=row(g_mem), w_kv_x=w_kv_x[l].astype(BF16),
        w_branch_a=w_branch_a[l].astype(BF16), w_branch_b=w_branch_b[l].astype(BF16),
        w_out=w_out[l].astype(BF16), w_q_x=w_q_x[l].astype(BF16), w_o_x=w_o_x[l].astype(BF16),
        g_hgrn_norm=row(g_hgrn_norm), g_post_mix=row(g_post_mix), g_pre_x=row(g_pre_x), g_post_x=row(g_post_x),
        g_pre_ffn=row(g_pre_ffn), w_gate_up=w_gate_up[l].astype(BF16), w_down=w_down[l].astype(BF16),
        g_post_ffn=row(g_post_ffn),
    )
    return (_layer(x_prompt, mem_prompt, w, tiles_p), _layer(x_sample, mem_sample, w, tiles_s))
```

```python
import functools
import math

import jax
import jax.numpy as jnp
import numpy as np
from jax import lax
from jax.experimental import pallas as pl
from jax.experimental.pallas import tpu as pltpu

F32 = jnp.float32
BF16 = jnp.bfloat16

D_MODEL = 2048
A_WIDTH = 1024
A_HALF = 64
A_HEADS = 8
A_VDIM = 128
B_WIDTH = 1024
B_DIM = 128
B_HEADS = 8
CHUNK = 64
REL_BUCKETS = 32
REL_MAX_DIST = 128
X_HEADS = 4
X_DIM = 128
X_WIDTH = 512
D_FF = 5632
N_IN = 12288
EPS = 1e-6
LOG2E = 1.4426950408889634
LAM_INIT = 0.8 - 0.6 * math.exp(-0.3 * 0)

LANES = 128
VT_ROWS = A_VDIM + 16
ATTN_BUFS = 2
VMEM_LIMIT = 56 * 1024 * 1024
PROJ_TN = 1024
FFN_TF = 512

COL_QA, COL_KA, COL_VA, COL_QB, COL_ZF, COL_ZB, COL_VB, COL_GB, COL_GATE = range(9)

NT = (((1,), (1,)), ((), ()))
TN = (((0,), (0,)), ((), ()))


def _rms(x, g):
    ms = jnp.mean(x * x, axis=-1, keepdims=True)
    return x * lax.rsqrt(ms + EPS) * g


def _sigmoid(x):
    return 1.0 / (1.0 + jnp.exp(-x))


def _inproj_kernel(x_ref, g_ref, w_ref, cb_ref, o_ref, h_ref):
    j = pl.program_id(1)

    @pl.when(j == 0)
    def _():
        h_ref[...] = _rms(x_ref[...], g_ref[...]).astype(BF16)

    @pl.when(j < COL_GATE)
    def _():
        scale = jnp.where(j == COL_QA, A_HALF ** -0.5 * LOG2E, jnp.where(j == COL_QB, B_DIM ** -0.5, 1.0))
        acc = jnp.dot(h_ref[...], w_ref[0], preferred_element_type=F32)
        o_ref[...] = (acc * scale).astype(BF16)

    @pl.when(j >= COL_GATE)
    def _():
        acc = jnp.dot(h_ref[...], w_ref[0], preferred_element_type=F32)
        o_ref[...] = _sigmoid(acc + cb_ref[...]).astype(BF16)


def _inproj(x2d, g, w, colbias, tm):
    t = x2d.shape[0]
    tn = PROJ_TN
    return pl.pallas_call(
        _inproj_kernel,
        grid=(t // tm, N_IN // tn),
        in_specs=[
            pl.BlockSpec((tm, D_MODEL), lambda i, j: (i, 0)),
            pl.BlockSpec((1, D_MODEL), lambda i, j: (0, 0)),
            pl.BlockSpec((1, D_MODEL, tn), lambda i, j: (j, 0, 0)),
            pl.BlockSpec((1, tn), lambda i, j: (0, j)),
        ],
        out_specs=pl.BlockSpec((tm, tn), lambda i, j: (i, j)),
        out_shape=jax.ShapeDtypeStruct((t, N_IN), BF16),
        scratch_shapes=[pltpu.VMEM((tm, D_MODEL), BF16)],
        compiler_params=pltpu.CompilerParams(
            dimension_semantics=("parallel", "arbitrary"), vmem_limit_bytes=VMEM_LIMIT),
        name="inproj",
    )(x2d, g, w, colbias)


def _memkv_kernel(m_ref, g_ref, w_ref, o_ref):
    h = _rms(m_ref[0], g_ref[...]).astype(BF16)
    o_ref[0] = jnp.dot(h, w_ref[...], preferred_element_type=F32).astype(BF16)


def _memkv(mem, g, w):
    b, n, _ = mem.shape
    return pl.pallas_call(
        _memkv_kernel,
        grid=(b,),
        in_specs=[
            pl.BlockSpec((1, n, D_MODEL), lambda i: (i, 0, 0)),
            pl.BlockSpec((1, D_MODEL), lambda i: (0, 0)),
            pl.BlockSpec((D_MODEL, 2 * X_WIDTH), lambda i: (0, 0)),
        ],
        out_specs=pl.BlockSpec((1, n, 2 * X_WIDTH), lambda i: (i, 0, 0)),
        out_shape=jax.ShapeDtypeStruct((b, n, 2 * X_WIDTH), BF16),
        compiler_params=pltpu.CompilerParams(
            dimension_semantics=("parallel",), vmem_limit_bytes=VMEM_LIMIT),
        name="memkv",
    )(mem, g, w)


def _attn_kernel(sc_ref, q_ref, k_ref, v_ref, strip_ref, g_ref, o_ref, vt_ref, *scratch, seq, tk, ns):
    bufs, qst_ref, acc_ref = scratch[:-2], scratch[-2], scratch[-1]
    h = pl.program_id(1)
    tq = ns * LANES
    nk = seq // tk
    nt = (seq // tq) * nk
    r = tk // LANES
    lam = sc_ref[h, 0]
    post = sc_ref[h, 1]

    row = lax.broadcasted_iota(jnp.int32, (LANES, LANES), 0)
    lane = lax.broadcasted_iota(jnp.int32, (LANES, LANES), 1)
    eye = (row == lane).astype(BF16)

    ones_row = (lax.broadcasted_iota(jnp.int32, (VT_ROWS - A_VDIM, tk), 0) == 0).astype(BF16)

    def vt_body(j, carry):
        vc = v_ref[0, pl.ds(pl.multiple_of(j * tk, tk), tk), :]
        vt_ref[j, :A_VDIM, :] = lax.dot_general(eye, vc, NT, preferred_element_type=F32).astype(BF16)
        vt_ref[j, A_VDIM:, :] = ones_row
        return carry

    lax.fori_loop(0, nk, vt_body, 0)

    def logits(t, dst_ref):
        qi = t // nk
        j = t - qi * nk
        r0 = jnp.clip((j * r - qi * ns) * LANES, -2 * tk, tk + tq)
        kc = k_ref[0, pl.ds(pl.multiple_of(j * tk, tk), tk), :]
        bias = strip_ref[0, pl.ds(pl.multiple_of(r0 + 2 * tk, LANES), tk), :]
        out = []
        for s in range(ns):
            qs = q_ref[0, pl.ds(pl.multiple_of(qi * tq + s * LANES, LANES), LANES), :]
            zero = jnp.zeros_like(qs)
            qst_ref[s] = jnp.concatenate([jnp.where(lane < A_HALF, qs, zero),
                                          jnp.where(lane >= A_HALF, qs, zero)], axis=0)
            b = bias[:, s * LANES:(s + 1) * LANES]
            st = (lax.dot_general(kc, qst_ref[s], NT, preferred_element_type=F32)
                  + jnp.concatenate([b, b], axis=1))
            dst_ref[s] = st
            part = jnp.max(st.reshape(4, tk // 4, 2 * LANES), axis=0)
            part = jnp.max(part.reshape(tk // 32, 8, 2 * LANES), axis=0)
            out.append(jnp.max(part, axis=0, keepdims=True))
        return tuple(out)

    def softmax(t, src_ref, dst_ref, ms, maxima):
        first = t % nk == 0
        out = []
        for s in range(ns):
            m = jnp.where(first, -1e30, ms[s])
            m_new = jnp.maximum(m, maxima[s])
            dst_ref[s] = jnp.exp2(src_ref[s] - m_new).astype(BF16)
            out.append((m_new, jnp.exp2(m - m_new)))
        return tuple(out)

    def values(t, p_ref, alphas, may_finish):
        qi = t // nk
        j = t - qi * nk
        vt = vt_ref[j]
        for s in range(ns):
            acc_ref[s] = alphas[s] * acc_ref[s] + jnp.dot(vt, p_ref[s], preferred_element_type=F32)

        if may_finish:
            @pl.when(j == nk - 1)
            def _():
                for s in range(ns):
                    acc = acc_ref[s]
                    o = acc[:A_VDIM] * (1.0 / acc[A_VDIM:A_VDIM + 1])
                    ot = o[:, :LANES] - lam * o[:, LANES:]
                    msq = jnp.mean(ot * ot, axis=0, keepdims=True)
                    y = ot * lax.rsqrt(msq + EPS) * g_ref[...] * post
                    o_ref[0, pl.ds(pl.multiple_of(qi * tq + s * LANES, LANES), LANES), :] = y.T.astype(BF16)

    s_refs = bufs[:ATTN_BUFS]
    p_refs = bufs[ATTN_BUFS:]

    def stage(t, k, carry, may_finish):
        ms, alphas, maxima = carry
        nxt_max = logits(jnp.minimum(t + 2, nt - 1), s_refs[(k + 2) % ATTN_BUFS])
        sm = softmax(jnp.minimum(t + 1, nt - 1), s_refs[(k + 1) % ATTN_BUFS], p_refs[(k + 1) % ATTN_BUFS],
                     ms, maxima)
        values(t, p_refs[k], alphas, may_finish)
        return tuple(x[0] for x in sm), tuple(x[1] for x in sm), nxt_max

    def body(u, carry):
        for k in range(ATTN_BUFS):
            carry = stage(ATTN_BUFS * u + k, k, carry, may_finish=(nk - 1 - k) % math.gcd(ATTN_BUFS, nk) == 0)
        return carry

    acc_ref[...] = jnp.zeros_like(acc_ref)
    max0 = logits(0, s_refs[0])
    max1 = logits(1, s_refs[1])
    sm0 = softmax(0, s_refs[0], p_refs[0], (jnp.full((1, 2 * LANES), -1e30, F32),) * ns, max0)
    init = (tuple(x[0] for x in sm0), tuple(x[1] for x in sm0), max1)
    lax.fori_loop(0, nt // ATTN_BUFS, body, init)


def _attention(p3, scal, strip, gcol, tk, ns):
    b, s, _ = p3.shape
    assert (s // (ns * LANES)) * (s // tk) % ATTN_BUFS == 0
    kern = functools.partial(_attn_kernel, seq=s, tk=tk, ns=ns)
    return pl.pallas_call(
        kern,
        grid=(b, A_HEADS),
        in_specs=[
            pl.BlockSpec(memory_space=pltpu.SMEM),
            pl.BlockSpec((1, s, LANES), lambda i, h: (i, 0, COL_QA * 8 + h)),
            pl.BlockSpec((1, s, LANES), lambda i, h: (i, 0, COL_KA * 8 + h)),
            pl.BlockSpec((1, s, LANES), lambda i, h: (i, 0, COL_VA * 8 + h)),
            pl.BlockSpec((1, 4 * tk + ns * LANES, ns * LANES), lambda i, h: (h, 0, 0)),
            pl.BlockSpec((A_VDIM, LANES), lambda i, h: (0, 0)),
        ],
        out_specs=pl.BlockSpec((1, s, LANES), lambda i, h: (i, 0, h)),
        out_shape=jax.ShapeDtypeStruct((b, s, A_WIDTH), BF16),
        scratch_shapes=([pltpu.VMEM((s // tk, VT_ROWS, tk), BF16)]
                        + [pltpu.VMEM((ns, tk, 2 * LANES), F32)] * ATTN_BUFS
                        + [pltpu.VMEM((ns, tk, 2 * LANES), BF16)] * ATTN_BUFS
                        + [pltpu.VMEM((ns, 2 * LANES, LANES), BF16)]
                        + [pltpu.VMEM((ns, VT_ROWS, 2 * LANES), F32)]),
        compiler_params=pltpu.CompilerParams(
            dimension_semantics=("parallel", "arbitrary"), vmem_limit_bytes=VMEM_LIMIT),
        name="diffattn",
    )(scal, p3, p3, p3, strip, gcol)


def _cumsum_mm(tri, x):
    hi = x.astype(BF16)
    lo = (x - hi.astype(F32)).astype(BF16)
    d = lambda y: jnp.dot(tri, y, preferred_element_type=F32)
    return d(hi) + d(lo)


def _hgrn_gates(q, z, c0, c1, tri, ref_row, last_row):
    f = c0 + c1 * jnp.tanh(0.5 * z)
    logf = jnp.log2(f)
    kk = 1.0 - f
    b = _cumsum_mm(tri, logf)
    b_ref = b[ref_row:ref_row + 1, :]
    b_last = b[last_row:last_row + 1, :]
    d = b - b_ref
    qd32 = q * jnp.exp2(d)
    kd32 = kk * jnp.exp2(-d)
    kl = (kd32 * jnp.exp2(b_last - b_ref)).astype(BF16)
    qe = (qd32 * jnp.exp2(b_ref)).astype(BF16)
    return qd32.astype(BF16), kd32.astype(BF16), kl, qe, jnp.exp2(b_last)


def _hgrn_heads(gates, vs, masks, st_refs):
    heads = [slice(h * B_DIM, (h + 1) * B_DIM) for h in range(B_HEADS)]
    n = len(gates)
    a = [[lax.dot_general(gates[i][0][:, sl], gates[i][1][:, sl], NT, preferred_element_type=F32) for sl in heads]
         for i in range(n)]
    st = [[st_refs[i][h] for h in range(B_HEADS)] for i in range(n)]
    inter = [[lax.dot_general(gates[i][3][:, sl], st[i][h].astype(BF16), NT, preferred_element_type=F32)
              for h, sl in enumerate(heads)] for i in range(n)]
    for i in range(n):
        for h, sl in enumerate(heads):
            upd = lax.dot_general(vs[i][:, sl], gates[i][2][:, sl], TN, preferred_element_type=F32)
            st_refs[i][h] = st[i][h] * gates[i][4][:, sl] + upd
    outs = []
    for i in range(n):
        o = [jnp.dot(jnp.where(masks[i], a[i][h], 0.0).astype(BF16), vs[i][:, sl], preferred_element_type=F32)
             + inter[i][h] for h, sl in enumerate(heads)]
        outs.append(jnp.concatenate(o, axis=1))
    return outs


def _hgrn_kernel(lb_ref, qf_ref, zf_ref, vf_ref, qb_ref, zb_ref, vb_ref, of_ref, ob_ref,
                 stf_ref, stb_ref, *, tc):
    @pl.when(pl.program_id(1) == 0)
    def _():
        stf_ref[...] = jnp.zeros_like(stf_ref)
        stb_ref[...] = jnp.zeros_like(stb_ref)

    nch = tc // CHUNK
    row = lax.broadcasted_iota(jnp.int32, (CHUNK, CHUNK), 0)
    col = lax.broadcasted_iota(jnp.int32, (CHUNK, CHUNK), 1)
    mask_f = col <= row
    mask_b = col >= row
    tri_f = mask_f.astype(BF16)
    tri_b = mask_b.astype(BF16)
    lb = lb_ref[...]
    c0 = 0.5 + 0.5 * lb
    c1 = 0.5 - 0.5 * lb

    def body(c, carry):
        rf = pl.ds(pl.multiple_of(c * CHUNK, CHUNK), CHUNK)
        rb = pl.ds(pl.multiple_of((nch - 1 - c) * CHUNK, CHUNK), CHUNK)
        ld = lambda ref, rows: ref[0, rows, :].astype(F32)
        gates_f = _hgrn_gates(ld(qf_ref, rf), ld(zf_ref, rf), c0[0:1], c1[0:1], tri_f, CHUNK // 2 - 1, CHUNK - 1)
        gates_b = _hgrn_gates(ld(qb_ref, rb), ld(zb_ref, rb), c0[1:2], c1[1:2], tri_b, CHUNK // 2, 0)
        out_f, out_b = _hgrn_heads((gates_f, gates_b), (vf_ref[0, rf, :], vb_ref[0, rb, :]),
                                   (mask_f, mask_b), (stf_ref, stb_ref))
        of_ref[0, rf, :] = out_f.astype(of_ref.dtype)
        ob_ref[0, rb, :] = out_b.astype(ob_ref.dtype)
        return carry

    lax.fori_loop(0, nch, body, 0, unroll=2)


def _hgrn(p3, lb2, tc):
    b, s, _ = p3.shape
    nb = s // tc
    fwd = lambda col: pl.BlockSpec((1, tc, B_WIDTH), lambda i, t: (i, t, col))
    bwd = lambda col: pl.BlockSpec((1, tc, B_WIDTH), lambda i, t: (i, nb - 1 - t, col))
    return pl.pallas_call(
        functools.partial(_hgrn_kernel, tc=tc),
        grid=(b, nb),
        in_specs=[
            pl.BlockSpec((2, B_WIDTH), lambda i, t: (0, 0)),
            fwd(COL_QB), fwd(COL_ZF), fwd(COL_VB),
            bwd(COL_QB), bwd(COL_ZB), bwd(COL_VB),
        ],
        out_specs=[
            pl.BlockSpec((1, tc, B_WIDTH), lambda i, t: (i, t, 0)),
            pl.BlockSpec((1, tc, B_WIDTH), lambda i, t: (i, nb - 1 - t, 0)),
        ],
        out_shape=[jax.ShapeDtypeStruct((b, s, B_WIDTH), BF16)] * 2,
        scratch_shapes=[pltpu.VMEM((B_HEADS, B_DIM, B_DIM), F32)] * 2,
        compiler_params=pltpu.CompilerParams(
            dimension_semantics=("parallel", "arbitrary"), vmem_limit_bytes=VMEM_LIMIT),
        name="hgrn2",
    )(lb2, p3, p3, p3, p3, p3, p3)


def _merge_kernel(x_ref, ya_ref, of_ref, ob_ref, gb_ref, gate_ref, kv_ref,
                  wa_ref, wb_ref, wo_ref, wq_ref, wox_ref,
                  ghg_ref, gpm_ref, gpx_ref, gpox_ref, o_ref):
    o = of_ref[...].astype(F32) + ob_ref[...].astype(F32)
    gb = gb_ref[...].astype(F32)
    ys = []
    for h in range(B_HEADS):
        sl = slice(h * B_DIM, (h + 1) * B_DIM)
        gh = gb[:, sl]
        ys.append((_rms(o[:, sl], ghg_ref[...]) * (gh * _sigmoid(gh))).astype(BF16))
    yb = jnp.concatenate(ys, axis=1)
    br_a = jnp.dot(ya_ref[...], wa_ref[...], preferred_element_type=F32)
    br_b = jnp.dot(yb, wb_ref[...], preferred_element_type=F32)
    g = gate_ref[...].astype(F32)
    mix = g[:, :D_MODEL] * br_a + g[:, D_MODEL:] * br_b
    t = jnp.dot(mix.astype(BF16), wo_ref[...], preferred_element_type=F32)
    x1 = x_ref[...] + _rms(t, gpm_ref[...])

    h2 = _rms(x1, gpx_ref[...]).astype(BF16)
    qx = (jnp.dot(h2, wq_ref[...], preferred_element_type=F32) * (X_DIM ** -0.5 * LOG2E)).astype(BF16)
    kv = kv_ref[0]
    heads = [slice(h * X_DIM, (h + 1) * X_DIM) for h in range(X_HEADS)]
    lgs = [lax.dot_general(qx[:, sl], kv[:, sl], NT, preferred_element_type=F32) for sl in heads]
    oxs = []
    for h, lg in enumerate(lgs):
        p = jnp.exp2(lg - jnp.max(lg, axis=-1, keepdims=True))
        l = jnp.sum(p, axis=-1, keepdims=True)
        vh = kv[:, X_WIDTH + h * X_DIM:X_WIDTH + (h + 1) * X_DIM]
        oxs.append((jnp.dot(p.astype(BF16), vh, preferred_element_type=F32) * (1.0 / l)).astype(BF16))
    ox = jnp.concatenate(oxs, axis=1)
    t2 = jnp.dot(ox, wox_ref[...], preferred_element_type=F32)
    o_ref[...] = x1 + _rms(t2, gpox_ref[...])


def _merge(x2d, ya2d, of2d, ob2d, p2d, kvx, wa, wb, wo, wq, wox, ghg, gpm, gpx, gpox, seq, tm):
    t = x2d.shape[0]
    per_seq = seq // tm
    const = lambda shape: pl.BlockSpec(shape, lambda i: (0,) * len(shape), pipeline_mode=pl.Buffered(1))
    return pl.pallas_call(
        _merge_kernel,
        grid=(t // tm,),
        in_specs=[
            pl.BlockSpec((tm, D_MODEL), lambda i: (i, 0)),
            pl.BlockSpec((tm, A_WIDTH), lambda i: (i, 0)),
            pl.BlockSpec((tm, B_WIDTH), lambda i: (i, 0)),
            pl.BlockSpec((tm, B_WIDTH), lambda i: (i, 0)),
            pl.BlockSpec((tm, B_WIDTH), lambda i: (i, COL_GB)),
            pl.BlockSpec((tm, 2 * D_MODEL), lambda i: (i, COL_GATE // 4)),
            pl.BlockSpec((1,) + kvx.shape[1:], lambda i: (i // per_seq, 0, 0)),
            const((A_WIDTH, D_MODEL)), const((B_WIDTH, D_MODEL)), const((D_MODEL, D_MODEL)),
            const((D_MODEL, X_WIDTH)), const((X_WIDTH, D_MODEL)),
            const((1, B_DIM)), const((1, D_MODEL)), const((1, D_MODEL)), const((1, D_MODEL)),
        ],
        out_specs=pl.BlockSpec((tm, D_MODEL), lambda i: (i, 0)),
        out_shape=jax.ShapeDtypeStruct((t, D_MODEL), F32),
        compiler_params=pltpu.CompilerParams(
            dimension_semantics=("parallel",), vmem_limit_bytes=VMEM_LIMIT),
        name="merge_xattn",
    )(x2d, ya2d, of2d, ob2d, p2d, p2d, kvx, wa, wb, wo, wq, wox, ghg, gpm, gpx, gpox)


def _ffn_kernel(x_ref, g_ref, wg_ref, wu_ref, wd_ref, gpost_ref, o_ref, h_ref, *, nf):
    f = pl.program_id(1)

    @pl.when(f == 0)
    def _():
        h_ref[...] = _rms(x_ref[...], g_ref[...]).astype(BF16)
        o_ref[...] = jnp.zeros_like(o_ref)

    h = h_ref[...]
    gt = jnp.dot(h, wg_ref[0], preferred_element_type=F32)
    up = jnp.dot(h, wu_ref[0], preferred_element_type=F32)
    act = (gt * _sigmoid(gt) * up).astype(BF16)
    o_ref[...] += jnp.dot(act, wd_ref[...], preferred_element_type=F32)

    @pl.when(f == nf - 1)
    def _():
        o_ref[...] = x_ref[...] + _rms(o_ref[...], gpost_ref[...])


def _ffn(x2d, g, wgu, wd, gpost, tm, tf):
    t = x2d.shape[0]
    nf = D_FF // tf
    return pl.pallas_call(
        functools.partial(_ffn_kernel, nf=nf),
        grid=(t // tm, nf),
        in_specs=[
            pl.BlockSpec((tm, D_MODEL), lambda i, f: (i, 0)),
            pl.BlockSpec((1, D_MODEL), lambda i, f: (0, 0)),
            pl.BlockSpec((1, D_MODEL, tf), lambda i, f: (f, 0, 0)),
            pl.BlockSpec((1, D_MODEL, tf), lambda i, f: (nf + f, 0, 0)),
            pl.BlockSpec((tf, D_MODEL), lambda i, f: (f, 0)),
            pl.BlockSpec((1, D_MODEL), lambda i, f: (0, 0)),
        ],
        out_specs=pl.BlockSpec((tm, D_MODEL), lambda i, f: (i, 0)),
        out_shape=jax.ShapeDtypeStruct((t, D_MODEL), F32),
        scratch_shapes=[pltpu.VMEM((tm, D_MODEL), BF16)],
        compiler_params=pltpu.CompilerParams(
            dimension_semantics=("parallel", "arbitrary"), vmem_limit_bytes=VMEM_LIMIT),
        name="swiglu",
    )(x2d, g, wgu, wgu, wd, gpost)


def _rel_bucket(rel):
    nb = REL_BUCKETS // 2
    max_exact = nb // 2
    n = jnp.abs(rel)
    nf = jnp.maximum(n, 1).astype(jnp.float32)
    large = max_exact + (jnp.log(nf / max_exact) / math.log(REL_MAX_DIST / max_exact) * (nb - max_exact)).astype(jnp.int32)
    large = jnp.minimum(large, nb - 1)
    return jnp.where(rel > 0, nb, 0) + jnp.where(n < max_exact, n, large)


def _tiles(seq):
    tk = min(1024, seq)
    ns = 2
    tc = min(256, seq)
    tm_proj = min(1024, seq)
    tm_merge = min(256, seq)
    tm_ffn = min(512, seq)
    return tk, ns, tc, tm_proj, tm_merge, tm_ffn


def _layer(x, mem, w, seq_tiles):
    b, s, _ = x.shape
    tk, ns, tc, tm_proj, tm_merge, tm_ffn = seq_tiles
    x2d = x.reshape(b * s, D_MODEL)
    p2d = _inproj(x2d, w["g_pre_mix"], w["w_in"], w["colbias"], tm_proj)
    p3 = p2d.reshape(b, s, N_IN)
    ya = _attention(p3, w["attn_scal"], w["strip"][(tk, ns)], w["g_subln_col"], tk, ns)
    o_f, o_b = _hgrn(p3, w["lb2"], tc)
    kvx = _memkv(mem, w["g_mem"], w["w_kv_x"])
    x2 = _merge(x2d, ya.reshape(b * s, A_WIDTH), o_f.reshape(b * s, B_WIDTH), o_b.reshape(b * s, B_WIDTH),
                p2d, kvx, w["w_branch_a"], w["w_branch_b"], w["w_out"], w["w_q_x"], w["w_o_x"],
                w["g_hgrn_norm"], w["g_post_mix"], w["g_pre_x"], w["g_post_x"], s, tm_merge)
    y = _ffn(x2, w["g_pre_ffn"], w["w_gate_up"], w["w_down"], w["g_post_ffn"], tm_ffn, FFN_TF)
    return y.reshape(b, s, D_MODEL)


def _col_chunks(w, width):
    k, n = w.shape
    return jnp.transpose(w.reshape(k, n // width, width), (1, 0, 2)).astype(BF16)


def _bias_strip(rel_bias, tk, tq):
    r = jnp.arange(-2 * tk, 2 * tk + tq, dtype=jnp.int32)[:, None]
    i = jnp.arange(tq, dtype=jnp.int32)[None, :]
    onehot = (_rel_bucket(r - i)[:, :, None] == jnp.arange(REL_BUCKETS, dtype=jnp.int32)).astype(F32)
    return jnp.einsum("rik,kh->hri", onehot, rel_bias.astype(F32) * LOG2E, precision=lax.Precision.HIGHEST)


def kernel(x_prompt, x_sample, mem_prompt, mem_sample, rel_bias, hgrn_lb_logits, g_pre_mix, w_in, b_merge,
           lam_q1, lam_k1, lam_q2, lam_k2, g_subln, g_hgrn_norm, w_branch_a, w_branch_b, w_out, g_post_mix,
           g_pre_x, g_mem, w_q_x, w_kv_x, w_o_x, g_post_x, g_pre_ffn, w_gate_up, w_down, g_post_ffn):
    l = 0
    row = lambda a: a[l].astype(F32).reshape(1, -1)
    lam = (jnp.exp(jnp.sum(lam_q1[l].astype(F32) * lam_k1[l].astype(F32)))
           - jnp.exp(jnp.sum(lam_q2[l].astype(F32) * lam_k2[l].astype(F32))) + LAM_INIT)
    lb_all = jnp.cumsum(jax.nn.softmax(hgrn_lb_logits.astype(F32), axis=1), axis=1)
    ones = jnp.ones((A_HEADS,), F32)
    attn_scal = jnp.stack([lam * ones, (1.0 - LAM_INIT) * ones], axis=1)
    tiles_p = _tiles(x_prompt.shape[1])
    tiles_s = _tiles(x_sample.shape[1])
    w = dict(
        g_pre_mix=row(g_pre_mix), w_in=_col_chunks(w_in[l], PROJ_TN),
        colbias=jnp.concatenate([jnp.zeros((1, N_IN - 2 * D_MODEL), F32), row(b_merge)], axis=1),
        attn_scal=attn_scal,
        strip={(tk, ns): _bias_strip(rel_bias, tk, ns * LANES) for tk, ns in {tiles_p[:2], tiles_s[:2]}},
        g_subln_col=jnp.broadcast_to(g_subln[l].astype(F32)[:, None], (A_VDIM, LANES)),
        lb2=jnp.stack([lb_all[0, l], lb_all[1, l]], axis=0),
        g_mem=row(g_mem), w_kv_x=w_kv_x[l].astype(BF16),
        w_branch_a=w_branch_a[l].astype(BF16), w_branch_b=w_branch_b[l].astype(BF16),
        w_out=w_out[l].astype(BF16), w_q_x=w_q_x[l].astype(BF16), w_o_x=w_o_x[l].astype(BF16),
        g_hgrn_norm=row(g_hgrn_norm), g_post_mix=row(g_post_mix), g_pre_x=row(g_pre_x), g_post_x=row(g_post_x),
        g_pre_ffn=row(g_pre_ffn), w_gate_up=_col_chunks(w_gate_up[l], FFN_TF), w_down=w_down[l].astype(BF16),
        g_post_ffn=row(g_post_ffn),
    )
    return (_layer(x_prompt, mem_prompt, w, tiles_p), _layer(x_sample, mem_sample, w, tiles_s))
```
